```python
import math
import jax, jax.numpy as jnp
from jax import lax
import numpy as np

D_MODEL = 1024
BATCH = 8
SEQ = 2048
DEPTH = 1

MEM_LEN = 256
HEAD_DIM = 64
SWA_HEADS = 8
SWA_KV_HEADS = 2
WINDOW = 128
BLOCK = 128
N_BUCKETS = 32
MAX_DISTANCE = 128
GLA_HEADS = 4
GLA_DK = 32
GLA_DV = 64
GLA_RANK = 16
GLA_TAU = 16.0
GLA_CHUNK = 32
MEM_HEADS = 4
D_FF = 2816
EPS = 1e-6

SWA_Q_W = SWA_HEADS * HEAD_DIM
SWA_KV_W = SWA_KV_HEADS * HEAD_DIM
GLA_QK_W = GLA_HEADS * GLA_DK
GLA_V_W = GLA_HEADS * GLA_DV
MEM_Q_W = MEM_HEADS * HEAD_DIM
MIX_W = SWA_Q_W + GLA_V_W + MEM_Q_W
IN_SIZES = (SWA_Q_W, SWA_KV_W, SWA_KV_W, GLA_QK_W, GLA_QK_W, GLA_V_W, GLA_V_W, GLA_RANK, MEM_Q_W)
IN_W = SWA_Q_W + 2 * SWA_KV_W + 2 * GLA_QK_W + 2 * GLA_V_W + GLA_RANK + MEM_Q_W

kernel_name = "hymba_swa_sink_gla_memxattn_macaron"


def _split_points():
    return [int(s) for s in np.cumsum(IN_SIZES)[:-1]]


def rmsnorm(x, g):
    x32 = x.astype(jnp.float32)
    y = x32 * lax.rsqrt(jnp.mean(x32 * x32, axis=-1, keepdims=True) + EPS)
    return (y * g.astype(jnp.float32)).astype(x.dtype)


def swiglu(x, w_gate, w_up, w_down):
    return (jax.nn.silu(x @ w_gate) * (x @ w_up)) @ w_down


def t5_bucket(dist):
    max_exact = N_BUCKETS // 2
    d = jnp.maximum(dist, 1).astype(jnp.float32)
    large = max_exact + (jnp.log(d / max_exact) / math.log(MAX_DISTANCE / max_exact)
                         * (N_BUCKETS - max_exact)).astype(jnp.int32)
    large = jnp.minimum(large, N_BUCKETS - 1)
    return jnp.where(dist < max_exact, dist, large)


def swa_attention(q, k, v, q_gain, k_gain, sinks, rel_bias):
    B, S = q.shape[0], q.shape[1]
    nb = S // BLOCK
    G = SWA_HEADS // SWA_KV_HEADS
    q = rmsnorm(q, q_gain)
    k = rmsnorm(k, k_gain)
    qb = q.reshape(B, nb, BLOCK, SWA_KV_HEADS, G, HEAD_DIM)
    pad = jnp.zeros((B, BLOCK, SWA_KV_HEADS, HEAD_DIM), k.dtype)

    def band(t):
        tp = jnp.concatenate([pad, t], axis=1).reshape(B, nb + 1, BLOCK, SWA_KV_HEADS, HEAD_DIM)
        return jnp.concatenate([tp[:, :-1], tp[:, 1:]], axis=2)

    kb, vb = band(k), band(v)
    scores = jnp.einsum('bnqhgd,bnkhd->bhgnqk', qb, kb).astype(jnp.float32) * (HEAD_DIM ** -0.5)

    qi = jnp.arange(BLOCK)[:, None]
    kj = jnp.arange(2 * BLOCK)[None, :]
    dist = qi + BLOCK - kj
    in_win = (dist >= 0) & (dist < WINDOW)
    key_pos = jnp.arange(nb)[:, None, None] * BLOCK - BLOCK + kj[None]
    valid = in_win[None] & (key_pos >= 0)

    bias = rel_bias[t5_bucket(jnp.maximum(dist, 0))].astype(jnp.float32)
    bias = bias.transpose(2, 0, 1).reshape(SWA_KV_HEADS, G, BLOCK, 2 * BLOCK)
    scores = scores + bias[:, :, None]
    scores = jnp.where(valid[None, None, None], scores, -jnp.inf)

    sink = sinks.astype(jnp.float32).reshape(SWA_KV_HEADS, G)[None, :, :, None, None, None]
    m = jnp.maximum(jnp.max(scores, axis=-1, keepdims=True), sink)
    p = jnp.exp(scores - m)
    p = p / (jnp.sum(p, axis=-1, keepdims=True) + jnp.exp(sink - m))
    out = jnp.einsum('bhgnqk,bnkhd->bnqhgd', p.astype(v.dtype), vb)
    return out.reshape(B, S, SWA_Q_W)


def gla(q, k, v, out_gate, gate_lr, w_gate_up, b_gate, norm_gain):
    B, S = q.shape[0], q.shape[1]
    N, C = S // GLA_CHUNK, GLA_CHUNK
    log_a = jax.nn.log_sigmoid((gate_lr @ w_gate_up + b_gate).astype(jnp.float32)) / GLA_TAU

    def chunked(t, d):
        return t.astype(jnp.float32).reshape(B, N, C, GLA_HEADS, d).transpose(0, 3, 1, 2, 4)

    qc = chunked(q, GLA_DK) * (GLA_DK ** -0.5)
    kc = chunked(k, GLA_DK)
    vc = chunked(v, GLA_DV)
    b = jnp.cumsum(chunked(log_a, GLA_DK), axis=3)

    causal = jnp.tril(jnp.ones((C, C), dtype=bool))
    diff = b[..., :, None, :] - b[..., None, :, :]
    decay = jnp.exp(jnp.where(causal[:, :, None], diff, -jnp.inf))
    A = jnp.sum(qc[..., :, None, :] * kc[..., None, :, :] * decay, axis=-1)
    o_intra = jnp.einsum('bhnij,bhnjd->bhnid', A, vc)

    b_last = b[..., -1:, :]
    chunk_kv = jnp.einsum('bhncd,bhnce->bhnde', kc * jnp.exp(b_last - b), vc)
    chunk_decay = jnp.exp(b_last[..., 0, :])

    def step(state, inp):
        dec, kv = inp
        return dec[..., None] * state + kv, state

    s0 = jnp.zeros((B, GLA_HEADS, GLA_DK, GLA_DV), jnp.float32)
    _, s_prev = lax.scan(step, s0, (jnp.moveaxis(chunk_decay, 2, 0), jnp.moveaxis(chunk_kv, 2, 0)))
    s_prev = jnp.moveaxis(s_prev, 0, 2)
    o_inter = jnp.einsum('bhncd,bhnde->bhnce', qc * jnp.exp(b), s_prev)

    o = (o_intra + o_inter).transpose(0, 2, 3, 1, 4).reshape(B, S, GLA_HEADS, GLA_DV)
    o = rmsnorm(o, norm_gain).reshape(B, S, GLA_V_W)
    return (o * jax.nn.silu(out_gate.astype(jnp.float32))).astype(q.dtype)


def memory_attention(q, k, v, q_gain, k_gain):
    B, S = q.shape[0], q.shape[1]
    q = rmsnorm(q, q_gain)
    k = rmsnorm(k, k_gain)
    s = jnp.einsum('bshd,bmhd->bhsm', q, k).astype(jnp.float32) * (HEAD_DIM ** -0.5)
    p = jax.nn.softmax(s, axis=-1)
    out = jnp.einsum('bhsm,bmhd->bshd', p.astype(v.dtype), v)
    return out.reshape(B, S, MEM_Q_W)


def _normal(k, shape, scale):
    return jax.random.normal(k, shape, jnp.float32) * scale


def _gain(k, shape):
    return 1.0 + 0.05 * jax.random.normal(k, shape, jnp.float32)


def setup_inputs(seed: int = 0) -> dict:
    key = jax.random.key(seed)
    ks = jax.random.split(key, 24)
    L, D = DEPTH, D_MODEL
    return {
        "x": _normal(ks[0], (BATCH, SEQ, D), 1.0),
        "mem": _normal(ks[1], (BATCH, MEM_LEN, D), 1.0),
        "ffn1_norm": _gain(ks[2], (L, D)),
        "ffn1_w_gate": _normal(ks[3], (L, D, D_FF), D ** -0.5),
        "ffn1_w_up": _normal(ks[4], (L, D, D_FF), D ** -0.5),
        "ffn1_w_down": _normal(ks[5], (L, D_FF, D), D_FF ** -0.5),
        "mix_norm": _gain(ks[6], (L, D)),
        "mem_norm": _gain(ks[7], (L, D)),
        "w_in": _normal(ks[8], (L, D, IN_W), D ** -0.5),
        "w_mem_kv": _normal(ks[9], (L, D, 2 * MEM_Q_W), D ** -0.5),
        "swa_q_norm": _gain(ks[10], (L, HEAD_DIM)),
        "swa_k_norm": _gain(ks[11], (L, HEAD_DIM)),
        "swa_sinks": _normal(ks[12], (L, SWA_HEADS), 0.5),
        "rel_bias": _normal(ks[13], (N_BUCKETS, SWA_HEADS), 0.2),
        "gla_w_gate_up": _normal(ks[14], (L, GLA_RANK, GLA_QK_W), GLA_RANK ** -0.5),
        "gla_b_gate": _normal(ks[15], (L, GLA_QK_W), 0.01),
        "gla_out_norm": _gain(ks[16], (L, GLA_DV)),
        "mem_q_norm": _gain(ks[17], (L, HEAD_DIM)),
        "mem_k_norm": _gain(ks[18], (L, HEAD_DIM)),
        "w_out": _normal(ks[19], (L, MIX_W, D), MIX_W ** -0.5),
        "ffn2_norm": _gain(ks[20], (L, D)),
        "ffn2_w_gate": _normal(ks[21], (L, D, D_FF), D ** -0.5),
        "ffn2_w_up": _normal(ks[22], (L, D, D_FF), D ** -0.5),
        "ffn2_w_down": _normal(ks[23], (L, D_FF, D), D_FF ** -0.5),
    }


def reference(x, mem, ffn1_norm, ffn1_w_gate, ffn1_w_up, ffn1_w_down, mix_norm, mem_norm,
              w_in, w_mem_kv, swa_q_norm, swa_k_norm, swa_sinks, rel_bias, gla_w_gate_up,
              gla_b_gate, gla_out_norm, mem_q_norm, mem_k_norm, w_out, ffn2_norm,
              ffn2_w_gate, ffn2_w_up, ffn2_w_down):
    B, S, _ = x.shape
    M = mem.shape[1]
    splits = _split_points()
    for l in range(DEPTH):
        x = x + 0.5 * swiglu(rmsnorm(x, ffn1_norm[l]), ffn1_w_gate[l], ffn1_w_up[l], ffn1_w_down[l])

        h = rmsnorm(x, mix_norm[l])
        sq, sk, sv, gq, gk, gv, gg, glr, mq = jnp.split(h @ w_in[l], splits, axis=-1)

        y_swa = swa_attention(sq.reshape(B, S, SWA_HEADS, HEAD_DIM),
                              sk.reshape(B, S, SWA_KV_HEADS, HEAD_DIM),
                              sv.reshape(B, S, SWA_KV_HEADS, HEAD_DIM),
                              swa_q_norm[l], swa_k_norm[l], swa_sinks[l], rel_bias)
        y_gla = gla(gq, gk, gv, gg, glr, gla_w_gate_up[l], gla_b_gate[l], gla_out_norm[l])
        mk, mv = jnp.split(rmsnorm(mem, mem_norm[l]) @ w_mem_kv[l], 2, axis=-1)
        y_mem = memory_attention(mq.reshape(B, S, MEM_HEADS, HEAD_DIM),
                                 mk.reshape(B, M, MEM_HEADS, HEAD_DIM),
                                 mv.reshape(B, M, MEM_HEADS, HEAD_DIM),
                                 mem_q_norm[l], mem_k_norm[l])
        x = x + jnp.concatenate([y_swa, y_gla, y_mem], axis=-1) @ w_out[l]

        x = x + 0.5 * swiglu(rmsnorm(x, ffn2_norm[l]), ffn2_w_gate[l], ffn2_w_up[l], ffn2_w_down[l])
    return x
```

```python
import functools
import math

import numpy as np
import jax
import jax.numpy as jnp
from jax import lax
from jax.experimental import pallas as pl
from jax.experimental.pallas import tpu as pltpu

F32 = jnp.float32
BF16 = jnp.bfloat16

D_MODEL = 1024
MEM_LEN = 256
HEAD_DIM = 64
SWA_HEADS = 8
SWA_KV_HEADS = 2
SWA_GROUP = SWA_HEADS // SWA_KV_HEADS
WINDOW = 128
BLOCK = 128
N_BUCKETS = 32
MAX_DISTANCE = 128
GLA_HEADS = 4
GLA_DK = 32
GLA_DV = 64
GLA_RANK = 16
GLA_TAU = 16.0
GLA_CHUNK = 32
MEM_HEADS = 4
D_FF = 2816
EPS = 1e-6

SWA_Q_W = SWA_HEADS * HEAD_DIM
SWA_KV_W = SWA_KV_HEADS * HEAD_DIM
GLA_QK_W = GLA_HEADS * GLA_DK
GLA_V_W = GLA_HEADS * GLA_DV
MEM_Q_W = MEM_HEADS * HEAD_DIM
MIX_W = SWA_Q_W + GLA_V_W + MEM_Q_W

LANES = 128

C_SQ = 0
C_SK = C_SQ + SWA_Q_W
C_SV = C_SK + SWA_KV_W
C_GQ = C_SV + SWA_KV_W
C_GK = C_GQ + GLA_QK_W
C_GV = C_GK + GLA_QK_W
C_GG = C_GV + GLA_V_W
C_MQ = C_GG + GLA_V_W
C_LR = C_MQ + MEM_Q_W
IN_W_PAD = C_LR + LANES

FFN_TM = 512
MIX_TQ = 512
VMEM_LIMIT = 56 * 1024 * 1024


def _t5_bucket_table():
    qi = np.arange(BLOCK)[:, None]
    kj = np.arange(2 * BLOCK)[None, :]
    dist = np.maximum(qi + BLOCK - kj, 0)
    max_exact = N_BUCKETS // 2
    d = np.maximum(dist, 1).astype(np.float64)
    val = np.log(d / max_exact) / math.log(MAX_DISTANCE / max_exact) * (N_BUCKETS - max_exact)
    in_win = (qi + BLOCK - kj >= 0) & (qi + BLOCK - kj < WINDOW)
    frac = np.abs(val - np.round(val))
    assert np.all((frac > 1e-6) | (d <= max_exact) | ~in_win)
    large = np.minimum(max_exact + np.floor(val).astype(np.int64), N_BUCKETS - 1)
    bkt = np.where(dist < max_exact, dist, large)
    bkt = np.where(in_win, bkt, -1)
    return bkt.astype(np.int32)


def _rmsnorm(x, g):
    ms = jnp.mean(x * x, axis=-1, keepdims=True)
    return x * lax.rsqrt(ms + EPS) * g


def _dot(a, b):
    return jnp.dot(a, b, preferred_element_type=F32)


def _dot_nt(a, b):
    return lax.dot_general(a, b, (((1,), (1,)), ((), ())), preferred_element_type=F32)


def _dot_tn(a, b):
    return lax.dot_general(a, b, (((0,), (0,)), ((), ())), preferred_element_type=F32)


def _headnorm(x, e_mat):
    x2 = x * x
    hi = x2.astype(BF16)
    lo = (x2 - hi.astype(F32)).astype(BF16)
    ms = _dot(hi, e_mat) + _dot(lo, e_mat)
    return x * lax.rsqrt(ms + EPS)


def _ffn_kernel(x_ref, g_ref, wg_ref, wu_ref, wd_ref, o_ref):
    x = x_ref[...]
    h = _rmsnorm(x, g_ref[...]).astype(BF16)
    g = _dot(h, wg_ref[...])
    u = _dot(h, wu_ref[...])
    a = (g * (1.0 / (1.0 + jnp.exp(-g)))) * u
    y = _dot(a.astype(BF16), wd_ref[...])
    o_ref[...] = x + 0.5 * y


def _const_spec(shape):
    return pl.BlockSpec(shape, lambda *_: (0,) * len(shape), pipeline_mode=pl.Buffered(1))


def _ffn(x2d, gain, wg, wu, wd):
    t = x2d.shape[0]
    return pl.pallas_call(
        _ffn_kernel,
        grid=(t // FFN_TM,),
        in_specs=[
            pl.BlockSpec((FFN_TM, D_MODEL), lambda i: (i, 0)),
            _const_spec((1, D_MODEL)),
            _const_spec((D_MODEL, D_FF)),
            _const_spec((D_MODEL, D_FF)),
            _const_spec((D_FF, D_MODEL)),
        ],
        out_specs=pl.BlockSpec((FFN_TM, D_MODEL), lambda i: (i, 0)),
        out_shape=jax.ShapeDtypeStruct((t, D_MODEL), F32),
        compiler_params=pltpu.CompilerParams(
            dimension_semantics=("arbitrary",), vmem_limit_bytes=VMEM_LIMIT),
        name="ffn_halfstep",
    )(x2d, gain, wg, wu, wd)


def _mix_kernel(sinks_ref, relb_ref,
                x_ref, mem_ref, mixg_ref, memg_ref, win_ref, wmem_ref, wout_ref,
                qg_ref, kg_ref, mqg_ref, mkg_ref, og_ref, wgu_ref, bg_ref, bkt_ref,
                o_ref,
                bias_scr, e_scr, tri_scr, ind_scr, smask_scr,
                kext_scr, vext_scr, mk_scr, mv_scr, st_scr, y_scr,
                gq_scr, gk_scr, gv_scr, b_scr, go_scr):
    bi = pl.program_id(0)
    ji = pl.program_id(1)
    tq = MIX_TQ
    nblk = tq // BLOCK
    nchunk = tq // GLA_CHUNK

    @pl.when((bi == 0) & (ji == 0))
    def _init_tables():
        bkt = bkt_ref[...]
        for h in range(SWA_HEADS):
            acc = jnp.zeros((BLOCK, 2 * BLOCK), F32)
            for k in range(N_BUCKETS):
                acc = jnp.where(bkt == k, relb_ref[k * SWA_HEADS + h], acc)
            bias_scr[h] = jnp.where(bkt >= 0, acc, -jnp.inf)
        r = lax.broadcasted_iota(jnp.int32, (512, 512), 0)
        c = lax.broadcasted_iota(jnp.int32, (512, 512), 1)
        e_scr[...] = jnp.where((r >> 6) == (c >> 6), 1.0 / HEAD_DIM, 0.0).astype(BF16)
        tri_scr[...] = jnp.where(((r >> 5) == (c >> 5)) & (c <= r), 1.0, 0.0).astype(BF16)
        r = lax.broadcasted_iota(jnp.int32, (GLA_QK_W, GLA_V_W), 0)
        c = lax.broadcasted_iota(jnp.int32, (GLA_QK_W, GLA_V_W), 1)
        ind_scr[...] = jnp.where((r >> 5) == (c >> 6), 1.0, 0.0).astype(BF16)
        r = lax.broadcasted_iota(jnp.int32, (GLA_V_W, GLA_QK_W), 0)
        c = lax.broadcasted_iota(jnp.int32, (GLA_V_W, GLA_QK_W), 1)
        smask_scr[...] = jnp.where((r >> 6) == (c >> 5), 1.0, 0.0).astype(F32)

    @pl.when(ji == 0)
    def _init_sequence():
        kext_scr[0:BLOCK, :] = jnp.zeros((BLOCK, SWA_KV_W), F32)
        vext_scr[0:BLOCK, :] = jnp.zeros((BLOCK, SWA_KV_W), F32)
        st_scr[...] = jnp.zeros(st_scr.shape, F32)
        hm = _rmsnorm(mem_ref[...], memg_ref[...]).astype(BF16)
        kv = _dot(hm, wmem_ref[...])
        mk = _headnorm(kv[:, :MEM_Q_W], e_scr[0:MEM_Q_W, 0:MEM_Q_W]) * mkg_ref[...]
        mk_scr[...] = mk.astype(BF16)
        mv_scr[...] = kv[:, MEM_Q_W:].astype(BF16)

    x = x_ref[...]
    h = _rmsnorm(x, mixg_ref[...]).astype(BF16)
    proj = _dot(h, win_ref[...])

    lane_lo = lax.broadcasted_iota(jnp.int32, (BLOCK, LANES), 1) < HEAD_DIM

    qn = _headnorm(proj[:, C_SQ:C_SQ + SWA_Q_W], e_scr[...]) * qg_ref[...]
    qn = (qn * (HEAD_DIM ** -0.5)).astype(BF16)
    kn = _headnorm(proj[:, C_SK:C_SK + SWA_KV_W], e_scr[0:SWA_KV_W, 0:SWA_KV_W]) * kg_ref[...]
    kext_scr[BLOCK:BLOCK + tq, :] = kn
    vext_scr[BLOCK:BLOCK + tq, :] = proj[:, C_SV:C_SV + SWA_KV_W]
    k_a = kext_scr[...]
    v_a = vext_scr[...]
    k_b = pltpu.roll(k_a, HEAD_DIM, 1)
    v_b = pltpu.roll(v_a, HEAD_DIM, 1)
    ext_lo = lax.broadcasted_iota(jnp.int32, k_a.shape, 1) < HEAD_DIM
    kk = (jnp.where(ext_lo, k_a, k_b).astype(BF16), jnp.where(ext_lo, k_b, k_a).astype(BF16))
    vv = (jnp.where(ext_lo, v_a, v_b).astype(BF16), jnp.where(ext_lo, v_b, v_a).astype(BF16))
    col = lax.broadcasted_iota(jnp.int32, (BLOCK, 2 * BLOCK), 1)
    zero_q = jnp.zeros((BLOCK, LANES), BF16)

    for nb in range(nblk):
        r0 = nb * BLOCK
        for hk in range(SWA_KV_HEADS):
            kband = kk[hk][r0:r0 + 2 * BLOCK]
            vband = vv[hk][r0:r0 + 2 * BLOCK]
            qparts = []
            for pair in range(2):
                c0 = hk * SWA_GROUP * HEAD_DIM + pair * LANES
                qp = qn[r0:r0 + BLOCK, c0:c0 + LANES]
                qparts.append(jnp.where(lane_lo, qp, zero_q))
                qparts.append(jnp.where(lane_lo, zero_q, qp))
            s_all = _dot_nt(jnp.concatenate(qparts, axis=0), kband)
            ps = []
            for g in range(SWA_GROUP):
                head = hk * SWA_GROUP + g
                s = s_all[g * BLOCK:(g + 1) * BLOCK] + bias_scr[head]
                if nb == 0:
                    s = jnp.where((col >= BLOCK) | (ji > 0), s, -jnp.inf)
                sink = sinks_ref[head]
                m = jnp.maximum(jnp.max(s, axis=-1, keepdims=True), sink)
                p = jnp.exp(s - m)
                den = jnp.sum(p, axis=-1, keepdims=True) + jnp.exp(sink - m)
                ps.append((p * (1.0 / den)).astype(BF16))
            o_all = _dot(jnp.concatenate(ps, axis=0), vband)
            for pair in range(2):
                o_pair = jnp.where(lane_lo, o_all[(2 * pair) * BLOCK:(2 * pair + 1) * BLOCK],
                                   o_all[(2 * pair + 1) * BLOCK:(2 * pair + 2) * BLOCK])
                c0 = hk * SWA_GROUP * HEAD_DIM + pair * LANES
                y_scr[r0:r0 + BLOCK, c0:c0 + LANES] = o_pair.astype(BF16)

    kext_scr[0:BLOCK, :] = kext_scr[tq:tq + BLOCK, :]
    vext_scr[0:BLOCK, :] = vext_scr[tq:tq + BLOCK, :]

    mqn = _headnorm(proj[:, C_MQ:C_MQ + MEM_Q_W], e_scr[0:MEM_Q_W, 0:MEM_Q_W]) * mqg_ref[...]
    mqn = (mqn * (HEAD_DIM ** -0.5)).astype(BF16)
    lane_lo_t = lax.broadcasted_iota(jnp.int32, (tq, LANES), 1) < HEAD_DIM
    zero_t = jnp.zeros((tq, LANES), BF16)
    for pair in range(MEM_HEADS // 2):
        qp = mqn[:, pair * LANES:(pair + 1) * LANES]
        qst = jnp.concatenate([jnp.where(lane_lo_t, qp, zero_t), jnp.where(lane_lo_t, zero_t, qp)], axis=0)
        s = _dot_nt(qst, mk_scr[:, pair * LANES:(pair + 1) * LANES])
        m = jnp.max(s, axis=-1, keepdims=True)
        p = jnp.exp(s - m)
        p = p * (1.0 / jnp.sum(p, axis=-1, keepdims=True))
        o2 = _dot(p.astype(BF16), mv_scr[:, pair * LANES:(pair + 1) * LANES])
        o_pair = jnp.where(lane_lo_t, o2[:tq], o2[tq:])
        c0 = SWA_Q_W + GLA_V_W + pair * LANES
        y_scr[:, c0:c0 + LANES] = o_pair.astype(BF16)

    z = _dot(proj[:, C_LR:C_LR + LANES].astype(BF16), wgu_ref[...]) + bg_ref[...]
    log_a = (jnp.minimum(z, 0.0) - jnp.log(1.0 + jnp.exp(-jnp.abs(z)))) * (1.0 / GLA_TAU)
    la1 = log_a.astype(BF16)
    rem = log_a - la1.astype(F32)
    la2 = rem.astype(BF16)
    la3 = (rem - la2.astype(F32)).astype(BF16)
    cs = _dot(tri_scr[...], jnp.concatenate([la1, la2, la3], axis=1))
    b_scr[...] = cs[:, 0:LANES] + cs[:, LANES:2 * LANES] + cs[:, 2 * LANES:3 * LANES]
    gq_scr[...] = proj[:, C_GQ:C_GQ + GLA_QK_W] * (GLA_DK ** -0.5)
    gk_scr[...] = proj[:, C_GK:C_GK + GLA_QK_W]
    gv_scr[...] = proj[:, C_GV:C_GV + GLA_V_W]
    ii = lax.broadcasted_iota(jnp.int32, (GLA_CHUNK, GLA_CHUNK, GLA_QK_W), 0)
    jj = lax.broadcasted_iota(jnp.int32, (GLA_CHUNK, GLA_CHUNK, GLA_QK_W), 1)
    causal = jj <= ii

    def chunk_body(c, carry):
        r = pl.multiple_of(c * GLA_CHUNK, GLA_CHUNK)
        bc = b_scr[pl.ds(r, GLA_CHUNK), :]
        qc = gq_scr[pl.ds(r, GLA_CHUNK), :]
        kc = gk_scr[pl.ds(r, GLA_CHUNK), :]
        vc = gv_scr[pl.ds(r, GLA_CHUNK), :]
        diff = bc[:, None, :] - bc[None, :, :]
        decay = jnp.exp(jnp.where(causal, diff, -jnp.inf))
        t3 = qc[:, None, :] * kc[None, :, :] * decay
        a_b = _dot(t3.reshape(GLA_CHUNK * GLA_CHUNK, GLA_QK_W).astype(BF16), ind_scr[...])
        o_intra = jnp.sum(a_b.reshape(GLA_CHUNK, GLA_CHUNK, GLA_V_W) * vc[None, :, :], axis=1)
        b_last = bc[GLA_CHUNK - 1:GLA_CHUNK, :]
        st = st_scr[...]
        o_inter = _dot_nt((qc * jnp.exp(bc)).astype(BF16), st.astype(BF16))
        kd = (kc * jnp.exp(b_last - bc)).astype(BF16)
        kv = _dot_tn(vc.astype(BF16), kd)
        st_scr[...] = jnp.exp(b_last) * st + kv * smask_scr[...]
        go_scr[pl.ds(r, GLA_CHUNK), :] = o_intra + o_inter
        return carry

    lax.fori_loop(0, nchunk, chunk_body, 0)

    gate = proj[:, C_GG:C_GG + GLA_V_W]
    on = _headnorm(go_scr[...], e_scr[0:GLA_V_W, 0:GLA_V_W]) * og_ref[...]
    yg = on * (gate * (1.0 / (1.0 + jnp.exp(-gate))))
    y_scr[:, SWA_Q_W:SWA_Q_W + GLA_V_W] = yg.astype(BF16)

    o_ref[...] = x + _dot(y_scr[...], wout_ref[...])


def _mix(x3d, mem, mix_g, mem_g, w_in_r, w_mem, w_out, qg, kg, mqg, mkg, og, wgu, bg,
         sinks, relb, bkt):
    bsz, seq, _ = x3d.shape
    tq = MIX_TQ
    smem = pl.BlockSpec(memory_space=pltpu.SMEM)
    in_specs = [
        smem, smem,
        pl.BlockSpec((None, tq, D_MODEL), lambda b, j: (b, j, 0)),
        pl.BlockSpec((None, MEM_LEN, D_MODEL), lambda b, j: (b, 0, 0)),
        _const_spec((1, D_MODEL)), _const_spec((1, D_MODEL)),
        _const_spec((D_MODEL, IN_W_PAD)),
        _const_spec((D_MODEL, 2 * MEM_Q_W)),
        _const_spec((MIX_W, D_MODEL)),
        _const_spec((1, SWA_Q_W)), _const_spec((1, SWA_KV_W)),
        _const_spec((1, MEM_Q_W)), _const_spec((1, MEM_Q_W)), _const_spec((1, GLA_V_W)),
        _const_spec((LANES, GLA_QK_W)), _const_spec((1, GLA_QK_W)),
        _const_spec((BLOCK, 2 * BLOCK)),
    ]
    scratch = [
        pltpu.VMEM((SWA_HEADS, BLOCK, 2 * BLOCK), F32),
        pltpu.VMEM((512, 512), BF16),
        pltpu.VMEM((tq, tq), BF16),
        pltpu.VMEM((GLA_QK_W, GLA_V_W), BF16),
        pltpu.VMEM((GLA_V_W, GLA_QK_W), F32),
        pltpu.VMEM((BLOCK + tq, SWA_KV_W), F32),
        pltpu.VMEM((BLOCK + tq, SWA_KV_W), F32),
        pltpu.VMEM((MEM_LEN, MEM_Q_W), BF16),
        pltpu.VMEM((MEM_LEN, MEM_Q_W), BF16),
        pltpu.VMEM((GLA_V_W, GLA_QK_W), F32),
        pltpu.VMEM((tq, MIX_W), BF16),
        pltpu.VMEM((tq, GLA_QK_W), F32),
        pltpu.VMEM((tq, GLA_QK_W), F32),
        pltpu.VMEM((tq, GLA_V_W), F32),
        pltpu.VMEM((tq, GLA_QK_W), F32),
        pltpu.VMEM((tq, GLA_V_W), F32),
    ]
    return pl.pallas_call(
        _mix_kernel,
        grid=(bsz, seq // tq),
        in_specs=in_specs,
        out_specs=pl.BlockSpec((None, tq, D_MODEL), lambda b, j: (b, j, 0)),
        out_shape=jax.ShapeDtypeStruct((bsz, seq, D_MODEL), F32),
        scratch_shapes=scratch,
        compiler_params=pltpu.CompilerParams(
            dimension_semantics=("arbitrary", "arbitrary"), vmem_limit_bytes=VMEM_LIMIT),
        name="mixing",
    )(sinks, relb, x3d, mem, mix_g, mem_g, w_in_r, w_mem, w_out, qg, kg, mqg, mkg, og, wgu, bg, bkt)


def _tile_gain(g, reps):
    return jnp.tile(g.astype(F32), reps).reshape(1, -1)


def kernel(x, mem, ffn1_norm, ffn1_w_gate, ffn1_w_up, ffn1_w_down, mix_norm, mem_norm, w_in, w_mem_kv, swa_q_norm, swa_k_norm, swa_sinks, rel_bias, gla_w_gate_up, gla_b_gate, gla_out_norm, mem_q_norm, mem_k_norm, w_out, ffn2_norm, ffn2_w_gate, ffn2_w_up, ffn2_w_down):
    bsz, seq, d = x.shape
    depth = ffn1_norm.shape[0]
    bkt = jnp.asarray(_t5_bucket_table())
    relb = rel_bias.astype(F32).reshape(-1)
    for l in range(depth):
        x2d = _ffn(x.reshape(bsz * seq, d), ffn1_norm[l].reshape(1, d),
                   ffn1_w_gate[l].astype(BF16), ffn1_w_up[l].astype(BF16), ffn1_w_down[l].astype(BF16))
        wl = w_in[l]
        lr0 = C_MQ
        w_in_r = jnp.concatenate(
            [wl[:, :lr0], wl[:, lr0 + GLA_RANK:], wl[:, lr0:lr0 + GLA_RANK],
             jnp.zeros((d, LANES - GLA_RANK), wl.dtype)], axis=1).astype(BF16)
        wgu = jnp.concatenate(
            [gla_w_gate_up[l], jnp.zeros((LANES - GLA_RANK, GLA_QK_W), gla_w_gate_up.dtype)],
            axis=0).astype(BF16)
        x3d = _mix(x2d.reshape(bsz, seq, d), mem, mix_norm[l].reshape(1, d), mem_norm[l].reshape(1, d),
                   w_in_r, w_mem_kv[l].astype(BF16), w_out[l].astype(BF16),
                   _tile_gain(swa_q_norm[l], SWA_HEADS), _tile_gain(swa_k_norm[l], SWA_KV_HEADS),
                   _tile_gain(mem_q_norm[l], MEM_HEADS), _tile_gain(mem_k_norm[l], MEM_HEADS),
                   _tile_gain(gla_out_norm[l], GLA_HEADS), wgu, gla_b_gate[l].reshape(1, -1),
                   swa_sinks[l].astype(F32), relb, bkt)
        x2d = _ffn(x3d.reshape(bsz * seq, d), ffn2_norm[l].reshape(1, d),
                   ffn2_w_gate[l].astype(BF16), ffn2_w_up[l].astype(BF16), ffn2_w_down[l].astype(BF16))
        x = x2d.reshape(bsz, seq, d)
    return x
```

```python
import functools
import math

import numpy as np
import jax
import jax.numpy as jnp
from jax import lax
from jax.experimental import pallas as pl
from jax.experimental.pallas import tpu as pltpu

F32 = jnp.float32
BF16 = jnp.bfloat16

D_MODEL = 1024
MEM_LEN = 256
HEAD_DIM = 64
SWA_HEADS = 8
SWA_KV_HEADS = 2
SWA_GROUP = SWA_HEADS // SWA_KV_HEADS
WINDOW = 128
BLOCK = 128
N_BUCKETS = 32
MAX_DISTANCE = 128
GLA_HEADS = 4
GLA_DK = 32
GLA_DV = 64
GLA_RANK = 16
GLA_TAU = 16.0
GLA_CHUNK = 32
MEM_HEADS = 4
D_FF = 2816
EPS = 1e-6

SWA_Q_W = SWA_HEADS * HEAD_DIM
SWA_KV_W = SWA_KV_HEADS * HEAD_DIM
GLA_QK_W = GLA_HEADS * GLA_DK
GLA_V_W = GLA_HEADS * GLA_DV
MEM_Q_W = MEM_HEADS * HEAD_DIM
MIX_W = SWA_Q_W + GLA_V_W + MEM_Q_W

LANES = 128
SUBLANES = 8

C_SQ = 0
C_SK = C_SQ + SWA_Q_W
C_SV = C_SK + SWA_KV_W
C_GQ = C_SV + SWA_KV_W
C_GK = C_GQ + GLA_QK_W
C_GV = C_GK + GLA_QK_W
C_GG = C_GV + GLA_V_W
C_MQ = C_GG + GLA_V_W
C_LR = C_MQ + MEM_Q_W
IN_W_PAD = C_LR + LANES

FFN_TM = 512
MIX_TQ = 512
VMEM_LIMIT = 56 * 1024 * 1024


def _t5_bucket_table():
    qi = np.arange(BLOCK)[:, None]
    kj = np.arange(2 * BLOCK)[None, :]
    dist = np.maximum(qi + BLOCK - kj, 0)
    max_exact = N_BUCKETS // 2
    d = np.maximum(dist, 1).astype(np.float64)
    val = np.log(d / max_exact) / math.log(MAX_DISTANCE / max_exact) * (N_BUCKETS - max_exact)
    in_win = (qi + BLOCK - kj >= 0) & (qi + BLOCK - kj < WINDOW)
    frac = np.abs(val - np.round(val))
    assert np.all((frac > 1e-6) | (d <= max_exact) | ~in_win)
    large = np.minimum(max_exact + np.floor(val).astype(np.int64), N_BUCKETS - 1)
    bkt = np.where(dist < max_exact, dist, large)
    bkt = np.where(in_win, bkt, -1)
    return bkt.astype(np.int32)


def _rmsnorm(x, g):
    ms = jnp.mean(x * x, axis=-1, keepdims=True)
    return x * lax.rsqrt(ms + EPS) * g


def _dot(a, b):
    return jnp.dot(a, b, preferred_element_type=F32)


def _dot_nt(a, b):
    return lax.dot_general(a, b, (((1,), (1,)), ((), ())), preferred_element_type=F32)


def _dot_tn(a, b):
    return lax.dot_general(a, b, (((0,), (0,)), ((), ())), preferred_element_type=F32)


def _headnorm(x, e_mat):
    x2 = x * x
    hi = x2.astype(BF16)
    lo = (x2 - hi.astype(F32)).astype(BF16)
    ms = _dot(hi, e_mat) + _dot(lo, e_mat)
    return x * lax.rsqrt(ms + EPS)


def _ffn_kernel(x_ref, g_ref, wg_ref, wu_ref, wd_ref, o_ref):
    x = x_ref[...]
    h = _rmsnorm(x, g_ref[...]).astype(BF16)
    g = _dot(h, wg_ref[...])
    u = _dot(h, wu_ref[...])
    a = (g * (1.0 / (1.0 + jnp.exp(-g)))) * u
    y = _dot(a.astype(BF16), wd_ref[...])
    o_ref[...] = x + 0.5 * y


def _const_spec(shape):
    return pl.BlockSpec(shape, lambda *_: (0,) * len(shape), pipeline_mode=pl.Buffered(1))


def _ffn(x2d, gain, wg, wu, wd):
    t = x2d.shape[0]
    return pl.pallas_call(
        _ffn_kernel,
        grid=(t // FFN_TM,),
        in_specs=[
            pl.BlockSpec((FFN_TM, D_MODEL), lambda i: (i, 0)),
            _const_spec((1, D_MODEL)),
            _const_spec((D_MODEL, D_FF)),
            _const_spec((D_MODEL, D_FF)),
            _const_spec((D_FF, D_MODEL)),
        ],
        out_specs=pl.BlockSpec((FFN_TM, D_MODEL), lambda i: (i, 0)),
        out_shape=jax.ShapeDtypeStruct((t, D_MODEL), F32),
        compiler_params=pltpu.CompilerParams(
            dimension_semantics=("arbitrary",), vmem_limit_bytes=VMEM_LIMIT),
        name="ffn_halfstep",
    )(x2d, gain, wg, wu, wd)


def _mix_kernel(sinks_ref, relb_ref,
                x_ref, mem_ref, mixg_ref, memg_ref, win_ref, wmem_ref, wout_ref,
                qg_ref, kg_ref, mqg_ref, mkg_ref, og_ref, wgu_ref, bg_ref, bkt_ref,
                o_ref,
                bias_scr, e_scr, tri_scr, ind_scr, smask_scr,
                kext_scr, vext_scr, mk_scr, mv_scr, st_scr, y_scr,
                gq_scr, gk_scr, gv_scr, b_scr, go_scr):
    bi = pl.program_id(0)
    ji = pl.program_id(1)
    tq = MIX_TQ
    nblk = tq // BLOCK
    nchunk = tq // GLA_CHUNK

    @pl.when((bi == 0) & (ji == 0))
    def _init_tables():
        bkt = bkt_ref[...]
        for h in range(SWA_HEADS):
            acc = jnp.zeros((BLOCK, 2 * BLOCK), F32)
            for k in range(N_BUCKETS):
                acc = jnp.where(bkt == k, relb_ref[k * SWA_HEADS + h], acc)
            bias_scr[h] = jnp.where(bkt >= 0, acc, -jnp.inf)
        r = lax.broadcasted_iota(jnp.int32, (512, 512), 0)
        c = lax.broadcasted_iota(jnp.int32, (512, 512), 1)
        e_scr[...] = jnp.where((r >> 6) == (c >> 6), 1.0 / HEAD_DIM, 0.0).astype(BF16)
        tri_scr[...] = jnp.where(((r >> 5) == (c >> 5)) & (c <= r), 1.0, 0.0).astype(BF16)
        r = lax.broadcasted_iota(jnp.int32, (GLA_QK_W, GLA_V_W), 0)
        c = lax.broadcasted_iota(jnp.int32, (GLA_QK_W, GLA_V_W), 1)
        ind_scr[...] = jnp.where((r >> 5) == (c >> 6), 1.0, 0.0).astype(BF16)
        r = lax.broadcasted_iota(jnp.int32, (GLA_V_W, GLA_QK_W), 0)
        c = lax.broadcasted_iota(jnp.int32, (GLA_V_W, GLA_QK_W), 1)
        smask_scr[...] = jnp.where((r >> 6) == (c >> 5), 1.0, 0.0).astype(F32)

    @pl.when(ji == 0)
    def _init_sequence():
        kext_scr[0:BLOCK, :] = jnp.zeros((BLOCK, SWA_KV_W), F32)
        vext_scr[0:BLOCK, :] = jnp.zeros((BLOCK, SWA_KV_W), F32)
        st_scr[...] = jnp.zeros(st_scr.shape, F32)
        hm = _rmsnorm(mem_ref[...], memg_ref[...]).astype(BF16)
        kv = _dot(hm, wmem_ref[...])
        mk = _headnorm(kv[:, :MEM_Q_W], e_scr[0:MEM_Q_W, 0:MEM_Q_W]) * mkg_ref[...]
        mk_scr[...] = mk.astype(BF16)
        mv_scr[...] = kv[:, MEM_Q_W:].astype(BF16)

    x = x_ref[...]
    h = _rmsnorm(x, mixg_ref[...]).astype(BF16)
    proj = _dot(h, win_ref[...])

    lane_lo = lax.broadcasted_iota(jnp.int32, (BLOCK, LANES), 1) < HEAD_DIM

    qn = _headnorm(proj[:, C_SQ:C_SQ + SWA_Q_W], e_scr[...]) * qg_ref[...]
    qn = (qn * (HEAD_DIM ** -0.5)).astype(BF16)
    kn = _headnorm(proj[:, C_SK:C_SK + SWA_KV_W], e_scr[0:SWA_KV_W, 0:SWA_KV_W]) * kg_ref[...]
    kext_scr[BLOCK:BLOCK + tq, :] = kn
    vext_scr[BLOCK:BLOCK + tq, :] = proj[:, C_SV:C_SV + SWA_KV_W]
    k_a = kext_scr[...]
    v_a = vext_scr[...]
    k_b = pltpu.roll(k_a, HEAD_DIM, 1)
    v_b = pltpu.roll(v_a, HEAD_DIM, 1)
    ext_lo = lax.broadcasted_iota(jnp.int32, k_a.shape, 1) < HEAD_DIM
    kk = (jnp.where(ext_lo, k_a, k_b).astype(BF16), jnp.where(ext_lo, k_b, k_a).astype(BF16))
    vv = (jnp.where(ext_lo, v_a, v_b).astype(BF16), jnp.where(ext_lo, v_b, v_a).astype(BF16))
    col = lax.broadcasted_iota(jnp.int32, (BLOCK, 2 * BLOCK), 1)
    zero_q = jnp.zeros((BLOCK, LANES), BF16)

    for nb in range(nblk):
        r0 = nb * BLOCK
        for hk in range(SWA_KV_HEADS):
            kband = kk[hk][r0:r0 + 2 * BLOCK]
            vband = vv[hk][r0:r0 + 2 * BLOCK]
            qparts = []
            for pair in range(2):
                c0 = hk * SWA_GROUP * HEAD_DIM + pair * LANES
                qp = qn[r0:r0 + BLOCK, c0:c0 + LANES]
                qparts.append(jnp.where(lane_lo, qp, zero_q))
                qparts.append(jnp.where(lane_lo, zero_q, qp))
            s_all = _dot_nt(jnp.concatenate(qparts, axis=0), kband)
            ps = []
            for g in range(SWA_GROUP):
                head = hk * SWA_GROUP + g
                s = s_all[g * BLOCK:(g + 1) * BLOCK] + bias_scr[head]
                if nb == 0:
                    s = jnp.where((col >= BLOCK) | (ji > 0), s, -jnp.inf)
                sink = sinks_ref[head]
                m = jnp.maximum(jnp.max(s, axis=-1, keepdims=True), sink)
                p = jnp.exp(s - m)
                den = jnp.sum(p, axis=-1, keepdims=True) + jnp.exp(sink - m)
                ps.append((p * (1.0 / den)).astype(BF16))
            o_all = _dot(jnp.concatenate(ps, axis=0), vband)
            for pair in range(2):
                o_pair = jnp.where(lane_lo, o_all[(2 * pair) * BLOCK:(2 * pair + 1) * BLOCK],
                                   o_all[(2 * pair + 1) * BLOCK:(2 * pair + 2) * BLOCK])
                c0 = hk * SWA_GROUP * HEAD_DIM + pair * LANES
                y_scr[r0:r0 + BLOCK, c0:c0 + LANES] = o_pair.astype(BF16)

    kext_scr[0:BLOCK, :] = kext_scr[tq:tq + BLOCK, :]
    vext_scr[0:BLOCK, :] = vext_scr[tq:tq + BLOCK, :]

    mqn = _headnorm(proj[:, C_MQ:C_MQ + MEM_Q_W], e_scr[0:MEM_Q_W, 0:MEM_Q_W]) * mqg_ref[...]
    mqn = (mqn * (HEAD_DIM ** -0.5)).astype(BF16)
    lane_lo_t = lax.broadcasted_iota(jnp.int32, (tq, LANES), 1) < HEAD_DIM
    zero_t = jnp.zeros((tq, LANES), BF16)
    for pair in range(MEM_HEADS // 2):
        qp = mqn[:, pair * LANES:(pair + 1) * LANES]
        qst = jnp.concatenate([jnp.where(lane_lo_t, qp, zero_t), jnp.where(lane_lo_t, zero_t, qp)], axis=0)
        s = _dot_nt(qst, mk_scr[:, pair * LANES:(pair + 1) * LANES])
        m = jnp.max(s, axis=-1, keepdims=True)
        p = jnp.exp(s - m)
        p = p * (1.0 / jnp.sum(p, axis=-1, keepdims=True))
        o2 = _dot(p.astype(BF16), mv_scr[:, pair * LANES:(pair + 1) * LANES])
        o_pair = jnp.where(lane_lo_t, o2[:tq], o2[tq:])
        c0 = SWA_Q_W + GLA_V_W + pair * LANES
        y_scr[:, c0:c0 + LANES] = o_pair.astype(BF16)

    z = _dot(proj[:, C_LR:C_LR + LANES].astype(BF16), wgu_ref[...]) + bg_ref[...]
    log_a = (jnp.minimum(z, 0.0) - jnp.log(1.0 + jnp.exp(-jnp.abs(z)))) * (1.0 / GLA_TAU)
    la1 = log_a.astype(BF16)
    rem = log_a - la1.astype(F32)
    la2 = rem.astype(BF16)
    la3 = (rem - la2.astype(F32)).astype(BF16)
    cs = _dot(tri_scr[...], jnp.concatenate([la1, la2, la3], axis=1))
    b_scr[...] = cs[:, 0:LANES] + cs[:, LANES:2 * LANES] + cs[:, 2 * LANES:3 * LANES]
    gq_scr[...] = proj[:, C_GQ:C_GQ + GLA_QK_W] * (GLA_DK ** -0.5)
    gk_scr[...] = proj[:, C_GK:C_GK + GLA_QK_W]
    gv_scr[...] = proj[:, C_GV:C_GV + GLA_V_W]
    sub_row = lax.broadcasted_iota(jnp.int32, (SUBLANES, GLA_QK_W), 0)

    def chunk_body(c, carry):
        r = pl.multiple_of(c * GLA_CHUNK, GLA_CHUNK)
        bc = b_scr[pl.ds(r, GLA_CHUNK), :]
        qc = gq_scr[pl.ds(r, GLA_CHUNK), :]
        kc = gk_scr[pl.ds(r, GLA_CHUNK), :]
        vc = gv_scr[pl.ds(r, GLA_CHUNK), :]
        nsub = GLA_CHUNK // SUBLANES
        qs = [qc[m * SUBLANES:(m + 1) * SUBLANES] for m in range(nsub)]
        bs = [bc[m * SUBLANES:(m + 1) * SUBLANES] for m in range(nsub)]
        terms = []
        for j in range(GLA_CHUNK):
            kj = kc[j:j + 1, :]
            bj = bc[j:j + 1, :]
            for m in range(j // SUBLANES, nsub):
                diff = bs[m] - bj
                if m == j // SUBLANES and j % SUBLANES:
                    diff = jnp.where(sub_row >= j % SUBLANES, diff, -jnp.inf)
                terms.append(qs[m] * kj * jnp.exp(diff))
        a_b = _dot(jnp.concatenate(terms, axis=0).astype(BF16), ind_scr[...])
        acc = [None] * nsub
        t = 0
        for j in range(GLA_CHUNK):
            vj = vc[j:j + 1, :]
            for m in range(j // SUBLANES, nsub):
                contrib = a_b[t * SUBLANES:(t + 1) * SUBLANES] * vj
                acc[m] = contrib if acc[m] is None else acc[m] + contrib
                t += 1
        o_intra = jnp.concatenate(acc, axis=0)
        b_last = bc[GLA_CHUNK - 1:GLA_CHUNK, :]
        st = st_scr[...]
        o_inter = _dot_nt((qc * jnp.exp(bc)).astype(BF16), st.astype(BF16))
        kd = (kc * jnp.exp(b_last - bc)).astype(BF16)
        kv = _dot_tn(vc.astype(BF16), kd)
        st_scr[...] = jnp.exp(b_last) * st + kv * smask_scr[...]
        go_scr[pl.ds(r, GLA_CHUNK), :] = o_intra + o_inter
        return carry

    lax.fori_loop(0, nchunk, chunk_body, 0, unroll=True)

    gate = proj[:, C_GG:C_GG + GLA_V_W]
    on = _headnorm(go_scr[...], e_scr[0:GLA_V_W, 0:GLA_V_W]) * og_ref[...]
    yg = on * (gate * (1.0 / (1.0 + jnp.exp(-gate))))
    y_scr[:, SWA_Q_W:SWA_Q_W + GLA_V_W] = yg.astype(BF16)

    o_ref[...] = x + _dot(y_scr[...], wout_ref[...])


def _mix(x3d, mem, mix_g, mem_g, w_in_r, w_mem, w_out, qg, kg, mqg, mkg, og, wgu, bg,
         sinks, relb, bkt):
    bsz, seq, _ = x3d.shape
    tq = MIX_TQ
    smem = pl.BlockSpec(memory_space=pltpu.SMEM)
    in_specs = [
        smem, smem,
        pl.BlockSpec((None, tq, D_MODEL), lambda b, j: (b, j, 0)),
        pl.BlockSpec((None, MEM_LEN, D_MODEL), lambda b, j: (b, 0, 0)),
        _const_spec((1, D_MODEL)), _const_spec((1, D_MODEL)),
        _const_spec((D_MODEL, IN_W_PAD)),
        _const_spec((D_MODEL, 2 * MEM_Q_W)),
        _const_spec((MIX_W, D_MODEL)),
        _const_spec((1, SWA_Q_W)), _const_spec((1, SWA_KV_W)),
        _const_spec((1, MEM_Q_W)), _const_spec((1, MEM_Q_W)), _const_spec((1, GLA_V_W)),
        _const_spec((LANES, GLA_QK_W)), _const_spec((1, GLA_QK_W)),
        _const_spec((BLOCK, 2 * BLOCK)),
    ]
    scratch = [
        pltpu.VMEM((SWA_HEADS, BLOCK, 2 * BLOCK), F32),
        pltpu.VMEM((512, 512), BF16),
        pltpu.VMEM((tq, tq), BF16),
        pltpu.VMEM((GLA_QK_W, GLA_V_W), BF16),
        pltpu.VMEM((GLA_V_W, GLA_QK_W), F32),
        pltpu.VMEM((BLOCK + tq, SWA_KV_W), F32),
        pltpu.VMEM((BLOCK + tq, SWA_KV_W), F32),
        pltpu.VMEM((MEM_LEN, MEM_Q_W), BF16),
        pltpu.VMEM((MEM_LEN, MEM_Q_W), BF16),
        pltpu.VMEM((GLA_V_W, GLA_QK_W), F32),
        pltpu.VMEM((tq, MIX_W), BF16),
        pltpu.VMEM((tq, GLA_QK_W), F32),
        pltpu.VMEM((tq, GLA_QK_W), F32),
        pltpu.VMEM((tq, GLA_V_W), F32),
        pltpu.VMEM((tq, GLA_QK_W), F32),
        pltpu.VMEM((tq, GLA_V_W), F32),
    ]
    return pl.pallas_call(
        _mix_kernel,
        grid=(bsz, seq // tq),
        in_specs=in_specs,
        out_specs=pl.BlockSpec((None, tq, D_MODEL), lambda b, j: (b, j, 0)),
        out_shape=jax.ShapeDtypeStruct((bsz, seq, D_MODEL), F32),
        scratch_shapes=scratch,
        compiler_params=pltpu.CompilerParams(
            dimension_semantics=("arbitrary", "arbitrary"), vmem_limit_bytes=VMEM_LIMIT),
        name="mixing",
    )(sinks, relb, x3d, mem, mix_g, mem_g, w_in_r, w_mem, w_out, qg, kg, mqg, mkg, og, wgu, bg, bkt)


def _tile_gain(g, reps):
    return jnp.tile(g.astype(F32), reps).reshape(1, -1)


def kernel(x, mem, ffn1_norm, ffn1_w_gate, ffn1_w_up, ffn1_w_down, mix_norm, mem_norm, w_in, w_mem_kv, swa_q_norm, swa_k_norm, swa_sinks, rel_bias, gla_w_gate_up, gla_b_gate, gla_out_norm, mem_q_norm, mem_k_norm, w_out, ffn2_norm, ffn2_w_gate, ffn2_w_up, ffn2_w_down):
    bsz, seq, d = x.shape
    depth = ffn1_norm.shape[0]
    bkt = jnp.asarray(_t5_bucket_table())
    relb = rel_bias.astype(F32).reshape(-1)
    for l in range(depth):
        x2d = _ffn(x.reshape(bsz * seq, d), ffn1_norm[l].reshape(1, d),
                   ffn1_w_gate[l].astype(BF16), ffn1_w_up[l].astype(BF16), ffn1_w_down[l].astype(BF16))
        wl = w_in[l]
        lr0 = C_MQ
        w_in_r = jnp.concatenate(
            [wl[:, :lr0], wl[:, lr0 + GLA_RANK:], wl[:, lr0:lr0 + GLA_RANK],
             jnp.zeros((d, LANES - GLA_RANK), wl.dtype)], axis=1).astype(BF16)
        wgu = jnp.concatenate(
            [gla_w_gate_up[l], jnp.zeros((LANES - GLA_RANK, GLA_QK_W), gla_w_gate_up.dtype)],
            axis=0).astype(BF16)
        x3d = _mix(x2d.reshape(bsz, seq, d), mem, mix_norm[l].reshape(1, d), mem_norm[l].reshape(1, d),
                   w_in_r, w_mem_kv[l].astype(BF16), w_out[l].astype(BF16),
                   _tile_gain(swa_q_norm[l], SWA_HEADS), _tile_gain(swa_k_norm[l], SWA_KV_HEADS),
                   _tile_gain(mem_q_norm[l], MEM_HEADS), _tile_gain(mem_k_norm[l], MEM_HEADS),
                   _tile_gain(gla_out_norm[l], GLA_HEADS), wgu, gla_b_gate[l].reshape(1, -1),
                   swa_sinks[l].astype(F32), relb, bkt)
        x2d = _ffn(x3d.reshape(bsz * seq, d), ffn2_norm[l].reshape(1, d),
                   ffn2_w_gate[l].astype(BF16), ffn2_w_up[l].astype(BF16), ffn2_w_down[l].astype(BF16))
        x = x2d.reshape(bsz, seq, d)
    return x
```

```python
import functools
import math

import numpy as np
import jax
import jax.numpy as jnp
from jax import lax
from jax.experimental import pallas as pl
from jax.experimental.pallas import tpu as pltpu

F32 = jnp.float32
BF16 = jnp.bfloat16

D_MODEL = 1024
MEM_LEN = 256
HEAD_DIM = 64
SWA_HEADS = 8
SWA_KV_HEADS = 2
SWA_GROUP = SWA_HEADS // SWA_KV_HEADS
WINDOW = 128
BLOCK = 128
N_BUCKETS = 32
MAX_DISTANCE = 128
GLA_HEADS = 4
GLA_DK = 32
GLA_DV = 64
GLA_RANK = 16
GLA_TAU = 16.0
GLA_CHUNK = 32
MEM_HEADS = 4
D_FF = 2816
EPS = 1e-6
LOG2E = math.log2(math.e)

SWA_Q_W = SWA_HEADS * HEAD_DIM
SWA_KV_W = SWA_KV_HEADS * HEAD_DIM
GLA_QK_W = GLA_HEADS * GLA_DK
GLA_V_W = GLA_HEADS * GLA_DV
MEM_Q_W = MEM_HEADS * HEAD_DIM
MIX_W = SWA_Q_W + GLA_V_W + MEM_Q_W

LANES = 128
SUBLANES = 8

C_SQ = 0
C_SK = C_SQ + SWA_Q_W
C_SV = C_SK + SWA_KV_W
C_GQ = C_SV + SWA_KV_W
C_GK = C_GQ + GLA_QK_W
C_GV = C_GK + GLA_QK_W
C_GG = C_GV + GLA_V_W
C_MQ = C_GG + GLA_V_W
C_LR = C_MQ + MEM_Q_W
IN_W_PAD = C_LR + LANES

FFN_TM = 512
MIX_TQ = 512
VMEM_LIMIT = 56 * 1024 * 1024


def _t5_bucket_table():
    qi = np.arange(BLOCK)[:, None]
    kj = np.arange(2 * BLOCK)[None, :]
    dist = np.maximum(qi + BLOCK - kj, 0)
    max_exact = N_BUCKETS // 2
    d = np.maximum(dist, 1).astype(np.float64)
    val = np.log(d / max_exact) / math.log(MAX_DISTANCE / max_exact) * (N_BUCKETS - max_exact)
    in_win = (qi + BLOCK - kj >= 0) & (qi + BLOCK - kj < WINDOW)
    frac = np.abs(val - np.round(val))
    assert np.all((frac > 1e-6) | (d <= max_exact) | ~in_win)
    large = np.minimum(max_exact + np.floor(val).astype(np.int64), N_BUCKETS - 1)
    bkt = np.where(dist < max_exact, dist, large)
    bkt = np.where(in_win, bkt, -1)
    return bkt.astype(np.int32)


def _rmsnorm(x, g):
    ms = jnp.mean(x * x, axis=-1, keepdims=True)
    return x * lax.rsqrt(ms + EPS) * g


def _dot(a, b):
    return jnp.dot(a, b, preferred_element_type=F32)


def _dot_nt(a, b):
    return lax.dot_general(a, b, (((1,), (1,)), ((), ())), preferred_element_type=F32)


def _dot_tn(a, b):
    return lax.dot_general(a, b, (((0,), (0,)), ((), ())), preferred_element_type=F32)


def _headnorm(x, e_mat):
    ms = _dot((x * x).astype(BF16), e_mat)
    return x * lax.rsqrt(ms + EPS)


def _ffn_kernel(x_ref, g_ref, wg_ref, wu_ref, wd_ref, o_ref):
    x = x_ref[...]
    h = _rmsnorm(x, g_ref[...]).astype(BF16)
    g = _dot(h, wg_ref[...])
    u = _dot(h, wu_ref[...])
    a = (g * (1.0 / (1.0 + jnp.exp(-g)))) * u
    y = _dot(a.astype(BF16), wd_ref[...])
    o_ref[...] = x + 0.5 * y


def _const_spec(shape):
    return pl.BlockSpec(shape, lambda *_: (0,) * len(shape), pipeline_mode=pl.Buffered(1))


def _ffn(x2d, gain, wg, wu, wd):
    t = x2d.shape[0]
    return pl.pallas_call(
        _ffn_kernel,
        grid=(t // FFN_TM,),
        in_specs=[
            pl.BlockSpec((FFN_TM, D_MODEL), lambda i: (i, 0)),
            _const_spec((1, D_MODEL)),
            _const_spec((D_MODEL, D_FF)),
            _const_spec((D_MODEL, D_FF)),
            _const_spec((D_FF, D_MODEL)),
        ],
        out_specs=pl.BlockSpec((FFN_TM, D_MODEL), lambda i: (i, 0)),
        out_shape=jax.ShapeDtypeStruct((t, D_MODEL), F32),
        compiler_params=pltpu.CompilerParams(
            dimension_semantics=("arbitrary",), vmem_limit_bytes=VMEM_LIMIT),
        name="ffn_halfstep",
    )(x2d, gain, wg, wu, wd)


def _mix_kernel(sinks_ref, relb_ref,
                x_ref, mem_ref, mixg_ref, memg_ref, win_ref, wmem_ref, wout_ref,
                qg_ref, kg_ref, mqg_ref, mkg_ref, og_ref, wgu_ref, bg_ref, bkt_ref,
                o_ref,
                bias_scr, e_scr, tri_scr, ind_scr, smask_scr,
                kext_scr, vext_scr, mk_scr, mv_scr, st_scr, y_scr,
                gq_scr, gk_scr, gv_scr, b_scr, go_scr):
    bi = pl.program_id(0)
    ji = pl.program_id(1)
    tq = MIX_TQ
    nblk = tq // BLOCK
    nchunk = tq // GLA_CHUNK

    @pl.when((bi == 0) & (ji == 0))
    def _init_tables():
        bkt = bkt_ref[...]
        for h in range(SWA_HEADS):
            acc = jnp.zeros((BLOCK, 2 * BLOCK), F32)
            for k in range(N_BUCKETS):
                acc = jnp.where(bkt == k, relb_ref[k * SWA_HEADS + h] * LOG2E, acc)
            bias_scr[h] = jnp.where(bkt >= 0, acc, -jnp.inf)
        r = lax.broadcasted_iota(jnp.int32, (512, 512), 0)
        c = lax.broadcasted_iota(jnp.int32, (512, 512), 1)
        e_scr[...] = jnp.where((r >> 6) == (c >> 6), 1.0 / HEAD_DIM, 0.0).astype(BF16)
        tri_scr[...] = jnp.where(((r >> 5) == (c >> 5)) & (c <= r), 1.0, 0.0).astype(BF16)
        r = lax.broadcasted_iota(jnp.int32, (GLA_QK_W, GLA_V_W), 0)
        c = lax.broadcasted_iota(jnp.int32, (GLA_QK_W, GLA_V_W), 1)
        ind_scr[...] = jnp.where((r >> 5) == (c >> 6), 1.0, 0.0).astype(BF16)
        r = lax.broadcasted_iota(jnp.int32, (GLA_V_W, GLA_QK_W), 0)
        c = lax.broadcasted_iota(jnp.int32, (GLA_V_W, GLA_QK_W), 1)
        smask_scr[...] = jnp.where((r >> 6) == (c >> 5), 1.0, 0.0).astype(F32)

    @pl.when(ji == 0)
    def _init_sequence():
        kext_scr[0:BLOCK, :] = jnp.zeros((BLOCK, SWA_KV_W), F32)
        vext_scr[0:BLOCK, :] = jnp.zeros((BLOCK, SWA_KV_W), F32)
        st_scr[...] = jnp.zeros(st_scr.shape, F32)
        hm = _rmsnorm(mem_ref[...], memg_ref[...]).astype(BF16)
        kv = _dot(hm, wmem_ref[...])
        mk = _headnorm(kv[:, :MEM_Q_W], e_scr[0:MEM_Q_W, 0:MEM_Q_W]) * mkg_ref[...]
        mk_scr[...] = mk.astype(BF16)
        mv_scr[...] = kv[:, MEM_Q_W:].astype(BF16)

    x = x_ref[...]
    h = _rmsnorm(x, mixg_ref[...]).astype(BF16)
    proj = _dot(h, win_ref[...])

    lane_lo = lax.broadcasted_iota(jnp.int32, (BLOCK, LANES), 1) < HEAD_DIM

    qn = _headnorm(proj[:, C_SQ:C_SQ + SWA_Q_W], e_scr[...]) * qg_ref[...]
    qn = (qn * (HEAD_DIM ** -0.5 * LOG2E)).astype(BF16)
    kn = _headnorm(proj[:, C_SK:C_SK + SWA_KV_W], e_scr[0:SWA_KV_W, 0:SWA_KV_W]) * kg_ref[...]
    kext_scr[BLOCK:BLOCK + tq, :] = kn
    vext_scr[BLOCK:BLOCK + tq, :] = proj[:, C_SV:C_SV + SWA_KV_W]
    k_a = kext_scr[...]
    v_a = vext_scr[...]
    k_b = pltpu.roll(k_a, HEAD_DIM, 1)
    v_b = pltpu.roll(v_a, HEAD_DIM, 1)
    ext_lo = lax.broadcasted_iota(jnp.int32, k_a.shape, 1) < HEAD_DIM
    kk = (jnp.where(ext_lo, k_a, k_b).astype(BF16), jnp.where(ext_lo, k_b, k_a).astype(BF16))
    vv = (jnp.where(ext_lo, v_a, v_b).astype(BF16), jnp.where(ext_lo, v_b, v_a).astype(BF16))
    col = lax.broadcasted_iota(jnp.int32, (BLOCK, 2 * BLOCK), 1)
    zero_q = jnp.zeros((BLOCK, LANES), BF16)

    for nb in range(nblk):
        r0 = nb * BLOCK
        for hk in range(SWA_KV_HEADS):
            kband = kk[hk][r0:r0 + 2 * BLOCK]
            vband = vv[hk][r0:r0 + 2 * BLOCK]
            qparts = []
            for pair in range(2):
                c0 = hk * SWA_GROUP * HEAD_DIM + pair * LANES
                qp = qn[r0:r0 + BLOCK, c0:c0 + LANES]
                qparts.append(jnp.where(lane_lo, qp, zero_q))
                qparts.append(jnp.where(lane_lo, zero_q, qp))
            s_all = _dot_nt(jnp.concatenate(qparts, axis=0), kband)
            ps, inv = [], []
            for g in range(SWA_GROUP):
                head = hk * SWA_GROUP + g
                s = s_all[g * BLOCK:(g + 1) * BLOCK] + bias_scr[head]
                if nb == 0:
                    s = jnp.where((col >= BLOCK) | (ji > 0), s, -jnp.inf)
                sink = sinks_ref[head] * LOG2E
                m = jnp.maximum(jnp.max(s, axis=-1, keepdims=True), sink)
                p = jnp.exp2(s - m)
                den = jnp.sum(p, axis=-1, keepdims=True) + jnp.exp2(sink - m)
                ps.append(p.astype(BF16))
                inv.append(1.0 / den)
            o_all = _dot(jnp.concatenate(ps, axis=0), vband)
            for pair in range(2):
                g0 = 2 * pair
                o_pair = jnp.where(lane_lo, o_all[g0 * BLOCK:(g0 + 1) * BLOCK] * inv[g0],
                                   o_all[(g0 + 1) * BLOCK:(g0 + 2) * BLOCK] * inv[g0 + 1])
                c0 = hk * SWA_GROUP * HEAD_DIM + pair * LANES
                y_scr[r0:r0 + BLOCK, c0:c0 + LANES] = o_pair.astype(BF16)

    kext_scr[0:BLOCK, :] = kext_scr[tq:tq + BLOCK, :]
    vext_scr[0:BLOCK, :] = vext_scr[tq:tq + BLOCK, :]

    mqn = _headnorm(proj[:, C_MQ:C_MQ + MEM_Q_W], e_scr[0:MEM_Q_W, 0:MEM_Q_W]) * mqg_ref[...]
    mqn = (mqn * (HEAD_DIM ** -0.5 * LOG2E)).astype(BF16)
    lane_lo_t = lax.broadcasted_iota(jnp.int32, (tq, LANES), 1) < HEAD_DIM
    zero_t = jnp.zeros((tq, LANES), BF16)
    for pair in range(MEM_HEADS // 2):
        qp = mqn[:, pair * LANES:(pair + 1) * LANES]
        qst = jnp.concatenate([jnp.where(lane_lo_t, qp, zero_t), jnp.where(lane_lo_t, zero_t, qp)], axis=0)
        s = _dot_nt(qst, mk_scr[:, pair * LANES:(pair + 1) * LANES])
        m = jnp.max(s, axis=-1, keepdims=True)
        p = jnp.exp2(s - m)
        rinv = 1.0 / jnp.sum(p, axis=-1, keepdims=True)
        o2 = _dot(p.astype(BF16), mv_scr[:, pair * LANES:(pair + 1) * LANES]) * rinv
        o_pair = jnp.where(lane_lo_t, o2[:tq], o2[tq:])
        c0 = SWA_Q_W + GLA_V_W + pair * LANES
        y_scr[:, c0:c0 + LANES] = o_pair.astype(BF16)

    z = _dot(proj[:, C_LR:C_LR + LANES].astype(BF16), wgu_ref[...]) + bg_ref[...]
    log_a = (jnp.minimum(z, 0.0) - jnp.log(1.0 + jnp.exp(-jnp.abs(z)))) * (1.0 / GLA_TAU)
    la1 = log_a.astype(BF16)
    rem = log_a - la1.astype(F32)
    la2 = rem.astype(BF16)
    la3 = (rem - la2.astype(F32)).astype(BF16)
    cs = _dot(tri_scr[...], jnp.concatenate([la1, la2, la3], axis=1))
    b_scr[...] = (cs[:, 0:LANES] + cs[:, LANES:2 * LANES] + cs[:, 2 * LANES:3 * LANES]) * LOG2E
    gq_scr[...] = proj[:, C_GQ:C_GQ + GLA_QK_W] * (GLA_DK ** -0.5)
    gk_scr[...] = proj[:, C_GK:C_GK + GLA_QK_W]
    gv_scr[...] = proj[:, C_GV:C_GV + GLA_V_W]
    sub_row = lax.broadcasted_iota(jnp.int32, (SUBLANES, GLA_QK_W), 0)

    def chunk_body(c, carry):
        r = pl.multiple_of(c * GLA_CHUNK, GLA_CHUNK)
        bc = b_scr[pl.ds(r, GLA_CHUNK), :]
        qc = gq_scr[pl.ds(r, GLA_CHUNK), :]
        kc = gk_scr[pl.ds(r, GLA_CHUNK), :]
        vc = gv_scr[pl.ds(r, GLA_CHUNK), :]
        nsub = GLA_CHUNK // SUBLANES
        qs = [qc[m * SUBLANES:(m + 1) * SUBLANES] for m in range(nsub)]
        bs = [bc[m * SUBLANES:(m + 1) * SUBLANES] for m in range(nsub)]
        terms = []
        for j in range(GLA_CHUNK):
            kj = kc[j:j + 1, :]
            bj = bc[j:j + 1, :]
            for m in range(j // SUBLANES, nsub):
                diff = bs[m] - bj
                if m == j // SUBLANES and j % SUBLANES:
                    diff = jnp.where(sub_row >= j % SUBLANES, diff, -jnp.inf)
                terms.append(qs[m] * kj * jnp.exp2(diff))
        a_b = _dot(jnp.concatenate(terms, axis=0).astype(BF16), ind_scr[...])
        acc = [None] * nsub
        t = 0
        for j in range(GLA_CHUNK):
            vj = vc[j:j + 1, :]
            for m in range(j // SUBLANES, nsub):
                contrib = a_b[t * SUBLANES:(t + 1) * SUBLANES] * vj
                acc[m] = contrib if acc[m] is None else acc[m] + contrib
                t += 1
        o_intra = jnp.concatenate(acc, axis=0)
        b_last = bc[GLA_CHUNK - 1:GLA_CHUNK, :]
        st = st_scr[...]
        o_inter = _dot_nt((qc * jnp.exp2(bc)).astype(BF16), st.astype(BF16))
        kd = (kc * jnp.exp2(b_last - bc)).astype(BF16)
        kv = _dot_tn(vc.astype(BF16), kd)
        st_scr[...] = jnp.exp2(b_last) * st + kv * smask_scr[...]
        go_scr[pl.ds(r, GLA_CHUNK), :] = o_intra + o_inter
        return carry

    lax.fori_loop(0, nchunk, chunk_body, 0, unroll=True)

    gate = proj[:, C_GG:C_GG + GLA_V_W]
    on = _headnorm(go_scr[...], e_scr[0:GLA_V_W, 0:GLA_V_W]) * og_ref[...]
    yg = on * (gate * (1.0 / (1.0 + jnp.exp(-gate))))
    y_scr[:, SWA_Q_W:SWA_Q_W + GLA_V_W] = yg.astype(BF16)

    o_ref[...] = x + _dot(y_scr[...], wout_ref[...])


def _mix(x3d, mem, mix_g, mem_g, w_in_r, w_mem, w_out, qg, kg, mqg, mkg, og, wgu, bg,
         sinks, relb, bkt):
    bsz, seq, _ = x3d.shape
    tq = MIX_TQ
    smem = pl.BlockSpec(memory_space=pltpu.SMEM)
    in_specs = [
        smem, smem,
        pl.BlockSpec((None, tq, D_MODEL), lambda b, j: (b, j, 0)),
        pl.BlockSpec((None, MEM_LEN, D_MODEL), lambda b, j: (b, 0, 0)),
        _const_spec((1, D_MODEL)), _const_spec((1, D_MODEL)),
        _const_spec((D_MODEL, IN_W_PAD)),
        _const_spec((D_MODEL, 2 * MEM_Q_W)),
        _const_spec((MIX_W, D_MODEL)),
        _const_spec((1, SWA_Q_W)), _const_spec((1, SWA_KV_W)),
        _const_spec((1, MEM_Q_W)), _const_spec((1, MEM_Q_W)), _const_spec((1, GLA_V_W)),
        _const_spec((LANES, GLA_QK_W)), _const_spec((1, GLA_QK_W)),
        _const_spec((BLOCK, 2 * BLOCK)),
    ]
    scratch = [
        pltpu.VMEM((SWA_HEADS, BLOCK, 2 * BLOCK), F32),
        pltpu.VMEM((512, 512), BF16),
        pltpu.VMEM((tq, tq), BF16),
        pltpu.VMEM((GLA_QK_W, GLA_V_W), BF16),
        pltpu.VMEM((GLA_V_W, GLA_QK_W), F32),
        pltpu.VMEM((BLOCK + tq, SWA_KV_W), F32),
        pltpu.VMEM((BLOCK + tq, SWA_KV_W), F32),
        pltpu.VMEM((MEM_LEN, MEM_Q_W), BF16),
        pltpu.VMEM((MEM_LEN, MEM_Q_W), BF16),
        pltpu.VMEM((GLA_V_W, GLA_QK_W), F32),
        pltpu.VMEM((tq, MIX_W), BF16),
        pltpu.VMEM((tq, GLA_QK_W), F32),
        pltpu.VMEM((tq, GLA_QK_W), F32),
        pltpu.VMEM((tq, GLA_V_W), F32),
        pltpu.VMEM((tq, GLA_QK_W), F32),
        pltpu.VMEM((tq, GLA_V_W), F32),
    ]
    return pl.pallas_call(
        _mix_kernel,
        grid=(bsz, seq // tq),
        in_specs=in_specs,
        out_specs=pl.BlockSpec((None, tq, D_MODEL), lambda b, j: (b, j, 0)),
        out_shape=jax.ShapeDtypeStruct((bsz, seq, D_MODEL), F32),
        scratch_shapes=scratch,
        compiler_params=pltpu.CompilerParams(
            dimension_semantics=("arbitrary", "arbitrary"), vmem_limit_bytes=VMEM_LIMIT),
        name="mixing",
    )(sinks, relb, x3d, mem, mix_g, mem_g, w_in_r, w_mem, w_out, qg, kg, mqg, mkg, og, wgu, bg, bkt)


def _tile_gain(g, reps):
    return jnp.tile(g.astype(F32), reps).reshape(1, -1)


def kernel(x, mem, ffn1_norm, ffn1_w_gate, ffn1_w_up, ffn1_w_down, mix_norm, mem_norm, w_in, w_mem_kv, swa_q_norm, swa_k_norm, swa_sinks, rel_bias, gla_w_gate_up, gla_b_gate, gla_out_norm, mem_q_norm, mem_k_norm, w_out, ffn2_norm, ffn2_w_gate, ffn2_w_up, ffn2_w_down):
    bsz, seq, d = x.shape
    depth = ffn1_norm.shape[0]
    bkt = jnp.asarray(_t5_bucket_table())
    relb = rel_bias.astype(F32).reshape(-1)
    for l in range(depth):
        x2d = _ffn(x.reshape(bsz * seq, d), ffn1_norm[l].reshape(1, d),
                   ffn1_w_gate[l].astype(BF16), ffn1_w_up[l].astype(BF16), ffn1_w_down[l].astype(BF16))
        wl = w_in[l]
        lr0 = C_MQ
        w_in_r = jnp.concatenate(
            [wl[:, :lr0], wl[:, lr0 + GLA_RANK:], wl[:, lr0:lr0 + GLA_RANK],
             jnp.zeros((d, LANES - GLA_RANK), wl.dtype)], axis=1).astype(BF16)
        wgu = jnp.concatenate(
            [gla_w_gate_up[l], jnp.zeros((LANES - GLA_RANK, GLA_QK_W), gla_w_gate_up.dtype)],
            axis=0).astype(BF16)
        x3d = _mix(x2d.reshape(bsz, seq, d), mem, mix_norm[l].reshape(1, d), mem_norm[l].reshape(1, d),
                   w_in_r, w_mem_kv[l].astype(BF16), w_out[l].astype(BF16),
                   _tile_gain(swa_q_norm[l], SWA_HEADS), _tile_gain(swa_k_norm[l], SWA_KV_HEADS),
                   _tile_gain(mem_q_norm[l], MEM_HEADS), _tile_gain(mem_k_norm[l], MEM_HEADS),
                   _tile_gain(gla_out_norm[l], GLA_HEADS), wgu, gla_b_gate[l].reshape(1, -1),
                   swa_sinks[l].astype(F32), relb, bkt)
        x2d = _ffn(x3d.reshape(bsz * seq, d), ffn2_norm[l].reshape(1, d),
                   ffn2_w_gate[l].astype(BF16), ffn2_w_up[l].astype(BF16), ffn2_w_down[l].astype(BF16))
        x = x2d.reshape(bsz, seq, d)
    return x
```

```python
import functools
import math

import numpy as np
import jax
import jax.numpy as jnp
from jax import lax
from jax.experimental import pallas as pl
from jax.experimental.pallas import tpu as pltpu

F32 = jnp.float32
BF16 = jnp.bfloat16

D_MODEL = 1024
MEM_LEN = 256
HEAD_DIM = 64
SWA_HEADS = 8
SWA_KV_HEADS = 2
SWA_GROUP = SWA_HEADS // SWA_KV_HEADS
WINDOW = 128
BLOCK = 128
N_BUCKETS = 32
MAX_DISTANCE = 128
GLA_HEADS = 4
GLA_DK = 32
GLA_DV = 64
GLA_RANK = 16
GLA_TAU = 16.0
GLA_CHUNK = 32
MEM_HEADS = 4
D_FF = 2816
EPS = 1e-6
LOG2E = math.log2(math.e)

SWA_Q_W = SWA_HEADS * HEAD_DIM
SWA_KV_W = SWA_KV_HEADS * HEAD_DIM
GLA_QK_W = GLA_HEADS * GLA_DK
GLA_V_W = GLA_HEADS * GLA_DV
MEM_Q_W = MEM_HEADS * HEAD_DIM
MIX_W = SWA_Q_W + GLA_V_W + MEM_Q_W

LANES = 128
SUBLANES = 8
BF16_ROWS = 16

C_SQ = 0
C_SK = C_SQ + SWA_Q_W
C_SV = C_SK + SWA_KV_W
C_GQ = C_SV + SWA_KV_W
C_GK = C_GQ + GLA_QK_W
C_GV = C_GK + GLA_QK_W
C_GG = C_GV + GLA_V_W
C_MQ = C_GG + GLA_V_W
C_LR = C_MQ + MEM_Q_W
IN_W_PAD = C_LR + LANES

FFN_TM = 512
MIX_TQ = 512
PIECE_W = 256
E_SIZE = max(SWA_Q_W, GLA_V_W, MEM_Q_W)
VMEM_LIMIT = 56 * 1024 * 1024


def _t5_bucket_table():
    qi = np.arange(BLOCK)[:, None]
    kj = np.arange(2 * BLOCK)[None, :]
    dist = np.maximum(qi + BLOCK - kj, 0)
    max_exact = N_BUCKETS // 2
    d = np.maximum(dist, 1).astype(np.float64)
    val = np.log(d / max_exact) / math.log(MAX_DISTANCE / max_exact) * (N_BUCKETS - max_exact)
    in_win = (qi + BLOCK - kj >= 0) & (qi + BLOCK - kj < WINDOW)
    frac = np.abs(val - np.round(val))
    assert np.all((frac > 1e-6) | (d <= max_exact) | ~in_win)
    large = np.minimum(max_exact + np.floor(val).astype(np.int64), N_BUCKETS - 1)
    bkt = np.where(dist < max_exact, dist, large)
    bkt = np.where(in_win, bkt, -1)
    return bkt.astype(np.int32)


def _rmsnorm(x, g):
    ms = jnp.mean(x * x, axis=-1, keepdims=True)
    return x * lax.rsqrt(ms + EPS) * g


def _dot(a, b):
    return jnp.dot(a, b, preferred_element_type=F32)


def _dot_nt(a, b):
    return lax.dot_general(a, b, (((1,), (1,)), ((), ())), preferred_element_type=F32)


def _dot_tn(a, b):
    return lax.dot_general(a, b, (((0,), (0,)), ((), ())), preferred_element_type=F32)


def _headnorm(x, e_mat):
    ms = _dot((x * x).astype(BF16), e_mat)
    return x * lax.rsqrt(ms + EPS)


def _silu(x):
    return x * (1.0 / (1.0 + jnp.exp(-x)))


def _ffn_kernel(x_ref, g_ref, wg_ref, wu_ref, wd_ref, *rest, n_cast, has_w_in):
    n_in = n_cast + int(has_w_in)
    cast_in, o_ref, cast_out = rest[:n_in], rest[n_in], rest[n_in + 1:]
    x = x_ref[...]
    h = _rmsnorm(x, g_ref[...]).astype(BF16)
    g = _dot(h, wg_ref[...])
    u = _dot(h, wu_ref[...])
    y = _dot((_silu(g) * u).astype(BF16), wd_ref[...])
    o_ref[...] = x + 0.5 * y
    for src, dst in zip(cast_in[:n_cast], cast_out[:n_cast]):
        dst[...] = src[...].astype(BF16)
    if has_w_in:
        src, dst = cast_in[n_cast], cast_out[n_cast]
        w = src[...]
        dst[:, 0:C_MQ] = w[:, 0:C_MQ].astype(BF16)
        dst[:, C_MQ:C_LR] = w[:, C_MQ + GLA_RANK:C_MQ + GLA_RANK + MEM_Q_W].astype(BF16)
        slab = w[:, C_MQ:C_MQ + LANES]
        lane = lax.broadcasted_iota(jnp.int32, slab.shape, 1)
        dst[:, C_LR:C_LR + LANES] = jnp.where(lane < GLA_RANK, slab, 0.0).astype(BF16)


def _const_spec(shape):
    return pl.BlockSpec(shape, lambda *_: (0,) * len(shape), pipeline_mode=pl.Buffered(1))


def _ffn(x2d, gain, wg, wu, wd, cast=(), w_in=None):
    t = x2d.shape[0]
    steps = t // FFN_TM
    extra_in, extra_specs, extra_out_specs, extra_out_shapes = [], [], [], []
    for w in tuple(cast) + ((w_in,) if w_in is not None else ()):
        rows, cols = w.shape
        nblk = steps
        while rows % nblk or (rows // nblk) % BF16_ROWS:
            nblk //= 2
        blk = rows // nblk
        out_cols = IN_W_PAD if w is w_in else cols
        extra_in.append(w)
        index_map = functools.partial(lambda i, n: (jnp.minimum(i, n - 1), 0), n=nblk)
        extra_specs.append(pl.BlockSpec((blk, cols), index_map))
        extra_out_specs.append(pl.BlockSpec((blk, out_cols), index_map))
        extra_out_shapes.append(jax.ShapeDtypeStruct((rows, out_cols), BF16))
    outs = pl.pallas_call(
        functools.partial(_ffn_kernel, n_cast=len(cast), has_w_in=w_in is not None),
        grid=(steps,),
        in_specs=[
            pl.BlockSpec((FFN_TM, D_MODEL), lambda i: (i, 0)),
            _const_spec((1, D_MODEL)),
            _const_spec((D_MODEL, D_FF)),
            _const_spec((D_MODEL, D_FF)),
            _const_spec((D_FF, D_MODEL)),
        ] + extra_specs,
        out_specs=[pl.BlockSpec((FFN_TM, D_MODEL), lambda i: (i, 0))] + extra_out_specs,
        out_shape=[jax.ShapeDtypeStruct((t, D_MODEL), F32)] + extra_out_shapes,
        compiler_params=pltpu.CompilerParams(
            dimension_semantics=("arbitrary",), vmem_limit_bytes=VMEM_LIMIT),
        name="ffn_halfstep",
    )(x2d, gain, wg, wu, wd, *extra_in)
    return outs[0], outs[1:]


def _mix_kernel(sinks_ref, relb_ref,
                xp_ref, xc_ref, xn_ref, mem_ref, mixg_ref, memg_ref, win_ref, wmem_ref, wout_ref,
                qg_ref, kg_ref, mqg_ref, mkg_ref, og_ref, wgu_ref, bg_ref, bkt_ref,
                o_ref,
                bias_scr, e_scr, tri_scr, ind_scr, smask_scr,
                proj_scr, kext_scr, vext_scr, mk_scr, mv_scr, st_scr, y_scr, yprev_scr,
                h_scr, gq_scr, gk_scr, gv_scr, gate_scr, b_scr, go_scr,
                *, tiles_per_seq):
    step = pl.program_id(0)
    first_of_seq = lax.rem(step, tiles_per_seq) == 0
    tq = MIX_TQ
    nblk = tq // BLOCK
    nchunk = tq // GLA_CHUNK

    @pl.when(step == 0)
    def _prologue():
        bkt = bkt_ref[...]
        for h in range(SWA_HEADS):
            acc = jnp.zeros((BLOCK, 2 * BLOCK), F32)
            for k in range(N_BUCKETS):
                acc = jnp.where(bkt == k, relb_ref[k * SWA_HEADS + h] * LOG2E, acc)
            bias_scr[h] = jnp.where(bkt >= 0, acc, -jnp.inf)
        r = lax.broadcasted_iota(jnp.int32, (E_SIZE, E_SIZE), 0)
        c = lax.broadcasted_iota(jnp.int32, (E_SIZE, E_SIZE), 1)
        e_scr[...] = jnp.where((r // HEAD_DIM) == (c // HEAD_DIM), 1.0 / HEAD_DIM, 0.0).astype(BF16)
        r = lax.broadcasted_iota(jnp.int32, (tq, tq), 0)
        c = lax.broadcasted_iota(jnp.int32, (tq, tq), 1)
        tri_scr[...] = jnp.where(((r // GLA_CHUNK) == (c // GLA_CHUNK)) & (c <= r), 1.0, 0.0).astype(BF16)
        r = lax.broadcasted_iota(jnp.int32, (GLA_QK_W, GLA_V_W), 0)
        c = lax.broadcasted_iota(jnp.int32, (GLA_QK_W, GLA_V_W), 1)
        ind_scr[...] = jnp.where((r // GLA_DK) == (c // GLA_DV), 1.0, 0.0).astype(BF16)
        r = lax.broadcasted_iota(jnp.int32, (GLA_V_W, GLA_QK_W), 0)
        c = lax.broadcasted_iota(jnp.int32, (GLA_V_W, GLA_QK_W), 1)
        smask_scr[...] = jnp.where((r // GLA_DV) == (c // GLA_DK), 1.0, 0.0).astype(F32)
        yprev_scr[...] = jnp.zeros(yprev_scr.shape, BF16)
        h0 = _rmsnorm(xc_ref[...], mixg_ref[...]).astype(BF16)
        proj_scr[...] = _dot(h0, win_ref[...])

    @pl.when(first_of_seq)
    def _init_sequence():
        kext_scr[0:BLOCK, :] = jnp.zeros((BLOCK, SWA_KV_W), F32)
        vext_scr[0:BLOCK, :] = jnp.zeros((BLOCK, SWA_KV_W), F32)
        st_scr[...] = jnp.zeros(st_scr.shape, F32)
        hm = _rmsnorm(mem_ref[...], memg_ref[...]).astype(BF16)
        kv = _dot(hm, wmem_ref[...])
        mk = _headnorm(kv[:, :MEM_Q_W], e_scr[0:MEM_Q_W, 0:MEM_Q_W]) * mkg_ref[...]
        mk_scr[...] = mk.astype(BF16)
        mv_scr[...] = kv[:, MEM_Q_W:].astype(BF16)

    qn = _headnorm(proj_scr[:, C_SQ:C_SQ + SWA_Q_W], e_scr[0:SWA_Q_W, 0:SWA_Q_W]) * qg_ref[...]
    qn = (qn * (HEAD_DIM ** -0.5 * LOG2E)).astype(BF16)
    kn = _headnorm(proj_scr[:, C_SK:C_SK + SWA_KV_W], e_scr[0:SWA_KV_W, 0:SWA_KV_W]) * kg_ref[...]
    kext_scr[BLOCK:BLOCK + tq, :] = kn
    vext_scr[BLOCK:BLOCK + tq, :] = proj_scr[:, C_SV:C_SV + SWA_KV_W]
    mqn = _headnorm(proj_scr[:, C_MQ:C_MQ + MEM_Q_W], e_scr[0:MEM_Q_W, 0:MEM_Q_W]) * mqg_ref[...]
    mqn = (mqn * (HEAD_DIM ** -0.5 * LOG2E)).astype(BF16)
    z = _dot(proj_scr[:, C_LR:C_LR + LANES].astype(BF16), wgu_ref[...]) + bg_ref[...]
    log_a = (jnp.minimum(z, 0.0) - jnp.log(1.0 + jnp.exp(-jnp.abs(z)))) * (1.0 / GLA_TAU)
    la1 = log_a.astype(BF16)
    rem = log_a - la1.astype(F32)
    la2 = rem.astype(BF16)
    la3 = (rem - la2.astype(F32)).astype(BF16)
    cs = _dot(tri_scr[...], jnp.concatenate([la1, la2, la3], axis=1))
    b_scr[...] = (cs[:, 0:LANES] + cs[:, LANES:2 * LANES] + cs[:, 2 * LANES:3 * LANES]) * LOG2E
    gq_scr[...] = proj_scr[:, C_GQ:C_GQ + GLA_QK_W] * (GLA_DK ** -0.5)
    gk_scr[...] = proj_scr[:, C_GK:C_GK + GLA_QK_W]
    gv_scr[...] = proj_scr[:, C_GV:C_GV + GLA_V_W]
    gate_scr[...] = proj_scr[:, C_GG:C_GG + GLA_V_W]

    h_scr[...] = _rmsnorm(xn_ref[...], mixg_ref[...]).astype(BF16)

    def project_piece(k):
        c0, c1 = k * PIECE_W, min((k + 1) * PIECE_W, IN_W_PAD)
        proj_scr[:, c0:c1] = _dot(h_scr[...], win_ref[:, c0:c1])

    def output_piece(k):
        c0, c1 = k * PIECE_W, (k + 1) * PIECE_W
        o_ref[:, c0:c1] = xp_ref[:, c0:c1] + _dot(yprev_scr[...], wout_ref[:, c0:c1])

    n_proj = -(-IN_W_PAD // PIECE_W)
    pieces = [functools.partial(project_piece, k) for k in range(n_proj)]
    pieces += [functools.partial(output_piece, k) for k in range(D_MODEL // PIECE_W)]

    def issue_pieces(n):
        for _ in range(n):
            if pieces:
                pieces.pop(0)()

    lane_lo = lax.broadcasted_iota(jnp.int32, (BLOCK, LANES), 1) < HEAD_DIM
    k_a = kext_scr[...]
    v_a = vext_scr[...]
    k_b = pltpu.roll(k_a, HEAD_DIM, 1)
    v_b = pltpu.roll(v_a, HEAD_DIM, 1)
    ext_lo = lax.broadcasted_iota(jnp.int32, k_a.shape, 1) < HEAD_DIM
    kk = (jnp.where(ext_lo, k_a, k_b).astype(BF16), jnp.where(ext_lo, k_b, k_a).astype(BF16))
    vv = (jnp.where(ext_lo, v_a, v_b).astype(BF16), jnp.where(ext_lo, v_b, v_a).astype(BF16))
    col = lax.broadcasted_iota(jnp.int32, (BLOCK, 2 * BLOCK), 1)
    zero_q = jnp.zeros((BLOCK, LANES), BF16)

    for nb in range(nblk):
        r0 = nb * BLOCK
        for hk in range(SWA_KV_HEADS):
            kband = kk[hk][r0:r0 + 2 * BLOCK]
            vband = vv[hk][r0:r0 + 2 * BLOCK]
            qparts = []
            for pair in range(2):
                c0 = hk * SWA_GROUP * HEAD_DIM + pair * LANES
                qp = qn[r0:r0 + BLOCK, c0:c0 + LANES]
                qparts.append(jnp.where(lane_lo, qp, zero_q))
                qparts.append(jnp.where(lane_lo, zero_q, qp))
            s_all = _dot_nt(jnp.concatenate(qparts, axis=0), kband)
            ps, inv = [], []
            for g in range(SWA_GROUP):
                head = hk * SWA_GROUP + g
                s = s_all[g * BLOCK:(g + 1) * BLOCK] + bias_scr[head]
                if nb == 0:
                    s = jnp.where((col >= BLOCK) | jnp.logical_not(first_of_seq), s, -jnp.inf)
                sink = sinks_ref[head] * LOG2E
                m = jnp.maximum(jnp.max(s, axis=-1, keepdims=True), sink)
                p = jnp.exp2(s - m)
                den = jnp.sum(p, axis=-1, keepdims=True) + jnp.exp2(sink - m)
                ps.append(p.astype(BF16))
                inv.append(1.0 / den)
            o_all = _dot(jnp.concatenate(ps, axis=0), vband)
            for pair in range(2):
                g0 = 2 * pair
                o_pair = jnp.where(lane_lo, o_all[g0 * BLOCK:(g0 + 1) * BLOCK] * inv[g0],
                                   o_all[(g0 + 1) * BLOCK:(g0 + 2) * BLOCK] * inv[g0 + 1])
                c0 = hk * SWA_GROUP * HEAD_DIM + pair * LANES
                y_scr[r0:r0 + BLOCK, c0:c0 + LANES] = o_pair.astype(BF16)

    kext_scr[0:BLOCK, :] = kext_scr[tq:tq + BLOCK, :]
    vext_scr[0:BLOCK, :] = vext_scr[tq:tq + BLOCK, :]

    lane_lo_t = lax.broadcasted_iota(jnp.int32, (tq, LANES), 1) < HEAD_DIM
    zero_t = jnp.zeros((tq, LANES), BF16)
    for pair in range(MEM_HEADS // 2):
        issue_pieces(2)
        qp = mqn[:, pair * LANES:(pair + 1) * LANES]
        qst = jnp.concatenate([jnp.where(lane_lo_t, qp, zero_t), jnp.where(lane_lo_t, zero_t, qp)], axis=0)
        s = _dot_nt(qst, mk_scr[:, pair * LANES:(pair + 1) * LANES])
        m = jnp.max(s, axis=-1, keepdims=True)
        p = jnp.exp2(s - m)
        rinv = 1.0 / jnp.sum(p, axis=-1, keepdims=True)
        o2 = _dot(p.astype(BF16), mv_scr[:, pair * LANES:(pair + 1) * LANES]) * rinv
        o_pair = jnp.where(lane_lo_t, o2[:tq], o2[tq:])
        c0 = SWA_Q_W + GLA_V_W + pair * LANES
        y_scr[:, c0:c0 + LANES] = o_pair.astype(BF16)

    sub_row = lax.broadcasted_iota(jnp.int32, (SUBLANES, GLA_QK_W), 0)
    nsub = GLA_CHUNK // SUBLANES

    for c in range(nchunk):
        if c % 2 == 0:
            issue_pieces(1)
        r = c * GLA_CHUNK
        bc = b_scr[r:r + GLA_CHUNK, :]
        qc = gq_scr[r:r + GLA_CHUNK, :]
        kc = gk_scr[r:r + GLA_CHUNK, :]
        vc = gv_scr[r:r + GLA_CHUNK, :]
        qs = [qc[m * SUBLANES:(m + 1) * SUBLANES] for m in range(nsub)]
        bs = [bc[m * SUBLANES:(m + 1) * SUBLANES] for m in range(nsub)]
        terms = []
        for j in range(GLA_CHUNK):
            kj = kc[j:j + 1, :]
            bj = bc[j:j + 1, :]
            for m in range(j // SUBLANES, nsub):
                diff = bs[m] - bj
                if m == j // SUBLANES and j % SUBLANES:
                    diff = jnp.where(sub_row >= j % SUBLANES, diff, -jnp.inf)
                terms.append(qs[m] * kj * jnp.exp2(diff))
        a_b = _dot(jnp.concatenate(terms, axis=0).astype(BF16), ind_scr[...])
        acc = [None] * nsub
        t = 0
        for j in range(GLA_CHUNK):
            vj = vc[j:j + 1, :]
            for m in range(j // SUBLANES, nsub):
                contrib = a_b[t * SUBLANES:(t + 1) * SUBLANES] * vj
                acc[m] = contrib if acc[m] is None else acc[m] + contrib
                t += 1
        o_intra = jnp.concatenate(acc, axis=0)
        b_last = bc[GLA_CHUNK - 1:GLA_CHUNK, :]
        st = st_scr[...]
        o_inter = _dot_nt((qc * jnp.exp2(bc)).astype(BF16), st.astype(BF16))
        kd = (kc * jnp.exp2(b_last - bc)).astype(BF16)
        kv = _dot_tn(vc.astype(BF16), kd)
        st_scr[...] = jnp.exp2(b_last) * st + kv * smask_scr[...]
        go_scr[r:r + GLA_CHUNK, :] = o_intra + o_inter

    issue_pieces(len(pieces))
    on = _headnorm(go_scr[...], e_scr[0:GLA_V_W, 0:GLA_V_W]) * og_ref[...]
    y_scr[:, SWA_Q_W:SWA_Q_W + GLA_V_W] = (on * _silu(gate_scr[...])).astype(BF16)

    yprev_scr[...] = y_scr[...]


def _mix(x2d, mem, mix_g, mem_g, w_in_r, w_mem, w_out, qg, kg, mqg, mkg, og, wgu, bg,
         sinks, relb, bkt, *, seq):
    tq = MIX_TQ
    ntiles = x2d.shape[0] // tq
    tiles_per_seq = seq // tq
    last = ntiles - 1
    smem = pl.BlockSpec(memory_space=pltpu.SMEM)
    in_specs = [
        smem, smem,
        pl.BlockSpec((tq, D_MODEL), lambda s: (jnp.maximum(s - 1, 0), 0)),
        pl.BlockSpec((tq, D_MODEL), lambda s: (jnp.minimum(s, last), 0)),
        pl.BlockSpec((tq, D_MODEL), lambda s: (jnp.minimum(s + 1, last), 0)),
        pl.BlockSpec((None, MEM_LEN, D_MODEL), lambda s: (jnp.minimum(s, last) // tiles_per_seq, 0, 0)),
        _const_spec((1, D_MODEL)), _const_spec((1, D_MODEL)),
        _const_spec((D_MODEL, IN_W_PAD)),
        _const_spec((D_MODEL, 2 * MEM_Q_W)),
        _const_spec((MIX_W, D_MODEL)),
        _const_spec((1, SWA_Q_W)), _const_spec((1, SWA_KV_W)),
        _const_spec((1, MEM_Q_W)), _const_spec((1, MEM_Q_W)), _const_spec((1, GLA_V_W)),
        _const_spec((LANES, GLA_QK_W)), _const_spec((1, GLA_QK_W)),
        _const_spec((BLOCK, 2 * BLOCK)),
    ]
    scratch = [
        pltpu.VMEM((SWA_HEADS, BLOCK, 2 * BLOCK), F32),
        pltpu.VMEM((E_SIZE, E_SIZE), BF16),
        pltpu.VMEM((tq, tq), BF16),
        pltpu.VMEM((GLA_QK_W, GLA_V_W), BF16),
        pltpu.VMEM((GLA_V_W, GLA_QK_W), F32),
        pltpu.VMEM((tq, IN_W_PAD), F32),
        pltpu.VMEM((BLOCK + tq, SWA_KV_W), F32),
        pltpu.VMEM((BLOCK + tq, SWA_KV_W), F32),
        pltpu.VMEM((MEM_LEN, MEM_Q_W), BF16),
        pltpu.VMEM((MEM_LEN, MEM_Q_W), BF16),
        pltpu.VMEM((GLA_V_W, GLA_QK_W), F32),
        pltpu.VMEM((tq, MIX_W), BF16),
        pltpu.VMEM((tq, MIX_W), BF16),
        pltpu.VMEM((tq, D_MODEL), BF16),
        pltpu.VMEM((tq, GLA_QK_W), F32),
        pltpu.VMEM((tq, GLA_QK_W), F32),
        pltpu.VMEM((tq, GLA_V_W), F32),
        pltpu.VMEM((tq, GLA_V_W), F32),
        pltpu.VMEM((tq, GLA_QK_W), F32),
        pltpu.VMEM((tq, GLA_V_W), F32),
    ]
    return pl.pallas_call(
        functools.partial(_mix_kernel, tiles_per_seq=tiles_per_seq),
        grid=(ntiles + 1,),
        in_specs=in_specs,
        out_specs=pl.BlockSpec((tq, D_MODEL), lambda s: (jnp.maximum(s - 1, 0), 0)),
        out_shape=jax.ShapeDtypeStruct(x2d.shape, F32),
        scratch_shapes=scratch,
        compiler_params=pltpu.CompilerParams(
            dimension_semantics=("arbitrary",), vmem_limit_bytes=VMEM_LIMIT),
        name="mixing",
    )(sinks, relb, x2d, x2d, x2d, mem, mix_g, mem_g, w_in_r, w_mem, w_out, qg, kg, mqg, mkg, og,
      wgu, bg, bkt)


def _tile_gain(g, reps):
    return jnp.tile(g.astype(F32), reps).reshape(1, -1)


def kernel(x, mem, ffn1_norm, ffn1_w_gate, ffn1_w_up, ffn1_w_down, mix_norm, mem_norm, w_in, w_mem_kv, swa_q_norm, swa_k_norm, swa_sinks, rel_bias, gla_w_gate_up, gla_b_gate, gla_out_norm, mem_q_norm, mem_k_norm, w_out, ffn2_norm, ffn2_w_gate, ffn2_w_up, ffn2_w_down):
    bsz, seq, d = x.shape
    depth = ffn1_norm.shape[0]
    bkt = jnp.asarray(_t5_bucket_table())
    relb = rel_bias.astype(F32).reshape(-1)
    x2d = x.reshape(bsz * seq, d)
    for l in range(depth):
        x2d, (wg2, wu2, wd2, w_out_b, w_mem_b, w_in_r) = _ffn(
            x2d, ffn1_norm[l].reshape(1, d),
            ffn1_w_gate[l].astype(BF16), ffn1_w_up[l].astype(BF16), ffn1_w_down[l].astype(BF16),
            cast=(ffn2_w_gate[l], ffn2_w_up[l], ffn2_w_down[l], w_out[l], w_mem_kv[l]), w_in=w_in[l])
        wgu = jnp.concatenate(
            [gla_w_gate_up[l], jnp.zeros((LANES - GLA_RANK, GLA_QK_W), gla_w_gate_up.dtype)],
            axis=0).astype(BF16)
        x2d = _mix(x2d, mem, mix_norm[l].reshape(1, d), mem_norm[l].reshape(1, d),
                   w_in_r, w_mem_b, w_out_b,
                   _tile_gain(swa_q_norm[l], SWA_HEADS), _tile_gain(swa_k_norm[l], SWA_KV_HEADS),
                   _tile_gain(mem_q_norm[l], MEM_HEADS), _tile_gain(mem_k_norm[l], MEM_HEADS),
                   _tile_gain(gla_out_norm[l], GLA_HEADS), wgu, gla_b_gate[l].reshape(1, -1),
                   swa_sinks[l].astype(F32), relb, bkt, seq=seq)
        x2d, _ = _ffn(x2d, ffn2_norm[l].reshape(1, d), wg2, wu2, wd2)
    return x2d.reshape(bsz, seq, d)
```

```python
import functools
import math

import numpy as np
import jax
import jax.numpy as jnp
from jax import lax
from jax.experimental import pallas as pl
from jax.experimental.pallas import tpu as pltpu

F32 = jnp.float32
BF16 = jnp.bfloat16

D_MODEL = 1024
MEM_LEN = 256
HEAD_DIM = 64
SWA_HEADS = 8
SWA_KV_HEADS = 2
SWA_GROUP = SWA_HEADS // SWA_KV_HEADS
WINDOW = 128
BLOCK = 128
N_BUCKETS = 32
MAX_DISTANCE = 128
GLA_HEADS = 4
GLA_DK = 32
GLA_DV = 64
GLA_RANK = 16
GLA_TAU = 16.0
GLA_CHUNK = 32
MEM_HEADS = 4
D_FF = 2816
EPS = 1e-6
LOG2E = math.log2(math.e)

SWA_Q_W = SWA_HEADS * HEAD_DIM
SWA_KV_W = SWA_KV_HEADS * HEAD_DIM
GLA_QK_W = GLA_HEADS * GLA_DK
GLA_V_W = GLA_HEADS * GLA_DV
MEM_Q_W = MEM_HEADS * HEAD_DIM
MIX_W = SWA_Q_W + GLA_V_W + MEM_Q_W

LANES = 128
SUBLANES = 8
BF16_ROWS = 16

C_SQ = 0
C_SK = C_SQ + SWA_Q_W
C_SV = C_SK + SWA_KV_W
C_GQ = C_SV + SWA_KV_W
C_GK = C_GQ + GLA_QK_W
C_GV = C_GK + GLA_QK_W
C_GG = C_GV + GLA_V_W
C_MQ = C_GG + GLA_V_W
C_LR = C_MQ + MEM_Q_W
IN_W_PAD = C_LR + LANES

FFN_TM = 1024
MIX_TQ = 512
PIECE_W = 256
E_SIZE = max(SWA_Q_W, GLA_V_W, MEM_Q_W)
VMEM_LIMIT = 56 * 1024 * 1024


def _t5_bucket_table():
    qi = np.arange(BLOCK)[:, None]
    kj = np.arange(2 * BLOCK)[None, :]
    dist = np.maximum(qi + BLOCK - kj, 0)
    max_exact = N_BUCKETS // 2
    d = np.maximum(dist, 1).astype(np.float64)
    val = np.log(d / max_exact) / math.log(MAX_DISTANCE / max_exact) * (N_BUCKETS - max_exact)
    in_win = (qi + BLOCK - kj >= 0) & (qi + BLOCK - kj < WINDOW)
    frac = np.abs(val - np.round(val))
    assert np.all((frac > 1e-6) | (d <= max_exact) | ~in_win)
    large = np.minimum(max_exact + np.floor(val).astype(np.int64), N_BUCKETS - 1)
    bkt = np.where(dist < max_exact, dist, large)
    bkt = np.where(in_win, bkt, -1)
    return bkt.astype(np.int32)


def _rmsnorm(x, g):
    ms = jnp.mean(x * x, axis=-1, keepdims=True)
    return x * lax.rsqrt(ms + EPS) * g


def _dot(a, b):
    return jnp.dot(a, b, preferred_element_type=F32)


def _dot_nt(a, b):
    return lax.dot_general(a, b, (((1,), (1,)), ((), ())), preferred_element_type=F32)


def _dot_tn(a, b):
    return lax.dot_general(a, b, (((0,), (0,)), ((), ())), preferred_element_type=F32)


def _headnorm(x, e_mat):
    ms = _dot((x * x).astype(BF16), e_mat)
    return x * lax.rsqrt(ms + EPS)


def _silu(x):
    return x * (1.0 / (1.0 + jnp.exp(-x)))


def _ffn_kernel(x_ref, g_ref, wg_ref, wu_ref, wd_ref, *rest, n_cast, has_w_in):
    n_in = n_cast + int(has_w_in)
    cast_in, o_ref, cast_out = rest[:n_in], rest[n_in], rest[n_in + 1:]
    x = x_ref[...]
    h = _rmsnorm(x, g_ref[...]).astype(BF16)
    g = _dot(h, wg_ref[...])
    u = _dot(h, wu_ref[...])
    y = _dot((_silu(g) * u).astype(BF16), wd_ref[...])
    o_ref[...] = x + 0.5 * y
    for src, dst in zip(cast_in[:n_cast], cast_out[:n_cast]):
        dst[...] = src[...].astype(BF16)
    if has_w_in:
        src, dst = cast_in[n_cast], cast_out[n_cast]
        w = src[...]
        dst[:, 0:C_MQ] = w[:, 0:C_MQ].astype(BF16)
        dst[:, C_MQ:C_LR] = w[:, C_MQ + GLA_RANK:C_MQ + GLA_RANK + MEM_Q_W].astype(BF16)
        slab = w[:, C_MQ:C_MQ + LANES]
        lane = lax.broadcasted_iota(jnp.int32, slab.shape, 1)
        dst[:, C_LR:C_LR + LANES] = jnp.where(lane < GLA_RANK, slab, 0.0).astype(BF16)


def _const_spec(shape):
    return pl.BlockSpec(shape, lambda *_: (0,) * len(shape), pipeline_mode=pl.Buffered(1))


def _ffn(x2d, gain, wg, wu, wd, cast=(), w_in=None):
    t = x2d.shape[0]
    steps = t // FFN_TM
    extra_in, extra_specs, extra_out_specs, extra_out_shapes = [], [], [], []
    for w in tuple(cast) + ((w_in,) if w_in is not None else ()):
        rows, cols = w.shape
        nblk = steps
        while rows % nblk or (rows // nblk) % BF16_ROWS:
            nblk //= 2
        blk = rows // nblk
        out_cols = IN_W_PAD if w is w_in else cols
        extra_in.append(w)
        index_map = functools.partial(lambda i, n: (jnp.minimum(i, n - 1), 0), n=nblk)
        extra_specs.append(pl.BlockSpec((blk, cols), index_map))
        extra_out_specs.append(pl.BlockSpec((blk, out_cols), index_map))
        extra_out_shapes.append(jax.ShapeDtypeStruct((rows, out_cols), BF16))
    outs = pl.pallas_call(
        functools.partial(_ffn_kernel, n_cast=len(cast), has_w_in=w_in is not None),
        grid=(steps,),
        in_specs=[
            pl.BlockSpec((FFN_TM, D_MODEL), lambda i: (i, 0)),
            _const_spec((1, D_MODEL)),
            _const_spec((D_MODEL, D_FF)),
            _const_spec((D_MODEL, D_FF)),
            _const_spec((D_FF, D_MODEL)),
        ] + extra_specs,
        out_specs=[pl.BlockSpec((FFN_TM, D_MODEL), lambda i: (i, 0))] + extra_out_specs,
        out_shape=[jax.ShapeDtypeStruct((t, D_MODEL), F32)] + extra_out_shapes,
        compiler_params=pltpu.CompilerParams(
            dimension_semantics=("arbitrary",), vmem_limit_bytes=VMEM_LIMIT),
        name="ffn_halfstep",
    )(x2d, gain, wg, wu, wd, *extra_in)
    return outs[0], outs[1:]


def _mix_kernel(sinks_ref, relb_ref,
                xp_ref, xc_ref, xn_ref, mem_ref, mixg_ref, memg_ref, win_ref, wmem_ref, wout_ref,
                qg_ref, kg_ref, mqg_ref, mkg_ref, og_ref, wgu_ref, bg_ref, bkt_ref,
                o_ref,
                bias_scr, e_scr, tri_scr, ind_scr, smask_scr,
                proj_scr, kext_scr, vext_scr, mk_scr, mv_scr, st_scr, y_scr, yprev_scr,
                h_scr, gq_scr, gk_scr, gv_scr, gate_scr, b_scr, go_scr,
                *, tiles_per_seq):
    step = pl.program_id(0)
    first_of_seq = lax.rem(step, tiles_per_seq) == 0
    tq = MIX_TQ
    nblk = tq // BLOCK
    nchunk = tq // GLA_CHUNK

    @pl.when(step == 0)
    def _prologue():
        bkt = bkt_ref[...]
        for h in range(SWA_HEADS):
            acc = jnp.zeros((BLOCK, 2 * BLOCK), F32)
            for k in range(N_BUCKETS):
                acc = jnp.where(bkt == k, relb_ref[k * SWA_HEADS + h] * LOG2E, acc)
            bias_scr[h] = jnp.where(bkt >= 0, acc, -jnp.inf)
        r = lax.broadcasted_iota(jnp.int32, (E_SIZE, E_SIZE), 0)
        c = lax.broadcasted_iota(jnp.int32, (E_SIZE, E_SIZE), 1)
        e_scr[...] = jnp.where((r // HEAD_DIM) == (c // HEAD_DIM), 1.0 / HEAD_DIM, 0.0).astype(BF16)
        r = lax.broadcasted_iota(jnp.int32, (tq, tq), 0)
        c = lax.broadcasted_iota(jnp.int32, (tq, tq), 1)
        tri_scr[...] = jnp.where(((r // GLA_CHUNK) == (c // GLA_CHUNK)) & (c <= r), 1.0, 0.0).astype(BF16)
        r = lax.broadcasted_iota(jnp.int32, (GLA_QK_W, GLA_V_W), 0)
        c = lax.broadcasted_iota(jnp.int32, (GLA_QK_W, GLA_V_W), 1)
        ind_scr[...] = jnp.where((r // GLA_DK) == (c // GLA_DV), 1.0, 0.0).astype(BF16)
        r = lax.broadcasted_iota(jnp.int32, (GLA_V_W, GLA_QK_W), 0)
        c = lax.broadcasted_iota(jnp.int32, (GLA_V_W, GLA_QK_W), 1)
        smask_scr[...] = jnp.where((r // GLA_DV) == (c // GLA_DK), 1.0, 0.0).astype(F32)
        yprev_scr[...] = jnp.zeros(yprev_scr.shape, BF16)
        h0 = _rmsnorm(xc_ref[...], mixg_ref[...]).astype(BF16)
        proj_scr[...] = _dot(h0, win_ref[...])

    @pl.when(first_of_seq)
    def _init_sequence():
        kext_scr[0:BLOCK, :] = jnp.zeros((BLOCK, SWA_KV_W), F32)
        vext_scr[0:BLOCK, :] = jnp.zeros((BLOCK, SWA_KV_W), F32)
        st_scr[...] = jnp.zeros(st_scr.shape, F32)
        hm = _rmsnorm(mem_ref[...], memg_ref[...]).astype(BF16)
        kv = _dot(hm, wmem_ref[...])
        mk = _headnorm(kv[:, :MEM_Q_W], e_scr[0:MEM_Q_W, 0:MEM_Q_W]) * mkg_ref[...]
        mk_scr[...] = mk.astype(BF16)
        mv_scr[...] = kv[:, MEM_Q_W:].astype(BF16)

    qn = _headnorm(proj_scr[:, C_SQ:C_SQ + SWA_Q_W], e_scr[0:SWA_Q_W, 0:SWA_Q_W]) * qg_ref[...]
    qn = (qn * (HEAD_DIM ** -0.5 * LOG2E)).astype(BF16)
    kn = _headnorm(proj_scr[:, C_SK:C_SK + SWA_KV_W], e_scr[0:SWA_KV_W, 0:SWA_KV_W]) * kg_ref[...]
    kext_scr[BLOCK:BLOCK + tq, :] = kn
    vext_scr[BLOCK:BLOCK + tq, :] = proj_scr[:, C_SV:C_SV + SWA_KV_W]
    mqn = _headnorm(proj_scr[:, C_MQ:C_MQ + MEM_Q_W], e_scr[0:MEM_Q_W, 0:MEM_Q_W]) * mqg_ref[...]
    mqn = (mqn * (HEAD_DIM ** -0.5 * LOG2E)).astype(BF16)
    z = _dot(proj_scr[:, C_LR:C_LR + LANES].astype(BF16), wgu_ref[...]) + bg_ref[...]
    log_a = (jnp.minimum(z, 0.0) - jnp.log(1.0 + jnp.exp(-jnp.abs(z)))) * (1.0 / GLA_TAU)
    la1 = log_a.astype(BF16)
    rem = log_a - la1.astype(F32)
    la2 = rem.astype(BF16)
    la3 = (rem - la2.astype(F32)).astype(BF16)
    cs = _dot(tri_scr[...], jnp.concatenate([la1, la2, la3], axis=1))
    b_scr[...] = (cs[:, 0:LANES] + cs[:, LANES:2 * LANES] + cs[:, 2 * LANES:3 * LANES]) * LOG2E
    gq_scr[...] = proj_scr[:, C_GQ:C_GQ + GLA_QK_W] * (GLA_DK ** -0.5)
    gk_scr[...] = proj_scr[:, C_GK:C_GK + GLA_QK_W]
    gv_scr[...] = proj_scr[:, C_GV:C_GV + GLA_V_W]
    gate_scr[...] = proj_scr[:, C_GG:C_GG + GLA_V_W]

    h_scr[...] = _rmsnorm(xn_ref[...], mixg_ref[...]).astype(BF16)

    def project_piece(k):
        c0, c1 = k * PIECE_W, min((k + 1) * PIECE_W, IN_W_PAD)
        proj_scr[:, c0:c1] = _dot(h_scr[...], win_ref[:, c0:c1])

    def output_piece(k):
        c0, c1 = k * PIECE_W, (k + 1) * PIECE_W
        o_ref[:, c0:c1] = xp_ref[:, c0:c1] + _dot(yprev_scr[...], wout_ref[:, c0:c1])

    n_proj = -(-IN_W_PAD // PIECE_W)
    pieces = [functools.partial(project_piece, k) for k in range(n_proj)]
    pieces += [functools.partial(output_piece, k) for k in range(D_MODEL // PIECE_W)]

    def issue_pieces(n):
        for _ in range(n):
            if pieces:
                pieces.pop(0)()

    lane_lo = lax.broadcasted_iota(jnp.int32, (BLOCK, LANES), 1) < HEAD_DIM
    k_a = kext_scr[...]
    v_a = vext_scr[...]
    k_b = pltpu.roll(k_a, HEAD_DIM, 1)
    v_b = pltpu.roll(v_a, HEAD_DIM, 1)
    ext_lo = lax.broadcasted_iota(jnp.int32, k_a.shape, 1) < HEAD_DIM
    kk = (jnp.where(ext_lo, k_a, k_b).astype(BF16), jnp.where(ext_lo, k_b, k_a).astype(BF16))
    vv = (jnp.where(ext_lo, v_a, v_b).astype(BF16), jnp.where(ext_lo, v_b, v_a).astype(BF16))
    col = lax.broadcasted_iota(jnp.int32, (BLOCK, 2 * BLOCK), 1)
    zero_q = jnp.zeros((BLOCK, LANES), BF16)

    for nb in range(nblk):
        r0 = nb * BLOCK
        for hk in range(SWA_KV_HEADS):
            kband = kk[hk][r0:r0 + 2 * BLOCK]
            vband = vv[hk][r0:r0 + 2 * BLOCK]
            qparts = []
            for pair in range(2):
                c0 = hk * SWA_GROUP * HEAD_DIM + pair * LANES
                qp = qn[r0:r0 + BLOCK, c0:c0 + LANES]
                qparts.append(jnp.where(lane_lo, qp, zero_q))
                qparts.append(jnp.where(lane_lo, zero_q, qp))
            s_all = _dot_nt(jnp.concatenate(qparts, axis=0), kband)
            ps, inv = [], []
            for g in range(SWA_GROUP):
                head = hk * SWA_GROUP + g
                s = s_all[g * BLOCK:(g + 1) * BLOCK] + bias_scr[head]
                if nb == 0:
                    s = jnp.where((col >= BLOCK) | jnp.logical_not(first_of_seq), s, -jnp.inf)
                sink = sinks_ref[head] * LOG2E
                m = jnp.maximum(jnp.max(s, axis=-1, keepdims=True), sink)
                p = jnp.exp2(s - m)
                den = jnp.sum(p, axis=-1, keepdims=True) + jnp.exp2(sink - m)
                ps.append(p.astype(BF16))
                inv.append(1.0 / den)
            o_all = _dot(jnp.concatenate(ps, axis=0), vband)
            for pair in range(2):
                g0 = 2 * pair
                o_pair = jnp.where(lane_lo, o_all[g0 * BLOCK:(g0 + 1) * BLOCK] * inv[g0],
                                   o_all[(g0 + 1) * BLOCK:(g0 + 2) * BLOCK] * inv[g0 + 1])
                c0 = hk * SWA_GROUP * HEAD_DIM + pair * LANES
                y_scr[r0:r0 + BLOCK, c0:c0 + LANES] = o_pair.astype(BF16)

    kext_scr[0:BLOCK, :] = kext_scr[tq:tq + BLOCK, :]
    vext_scr[0:BLOCK, :] = vext_scr[tq:tq + BLOCK, :]

    lane_lo_t = lax.broadcasted_iota(jnp.int32, (tq, LANES), 1) < HEAD_DIM
    zero_t = jnp.zeros((tq, LANES), BF16)
    for pair in range(MEM_HEADS // 2):
        issue_pieces(2)
        qp = mqn[:, pair * LANES:(pair + 1) * LANES]
        qst = jnp.concatenate([jnp.where(lane_lo_t, qp, zero_t), jnp.where(lane_lo_t, zero_t, qp)], axis=0)
        s = _dot_nt(qst, mk_scr[:, pair * LANES:(pair + 1) * LANES])
        m = jnp.max(s, axis=-1, keepdims=True)
        p = jnp.exp2(s - m)
        rinv = 1.0 / jnp.sum(p, axis=-1, keepdims=True)
        o2 = _dot(p.astype(BF16), mv_scr[:, pair * LANES:(pair + 1) * LANES]) * rinv
        o_pair = jnp.where(lane_lo_t, o2[:tq], o2[tq:])
        c0 = SWA_Q_W + GLA_V_W + pair * LANES
        y_scr[:, c0:c0 + LANES] = o_pair.astype(BF16)

    sub_row = lax.broadcasted_iota(jnp.int32, (SUBLANES, GLA_QK_W), 0)
    nsub = GLA_CHUNK // SUBLANES

    for c in range(nchunk):
        if c % 2 == 0:
            issue_pieces(1)
        r = c * GLA_CHUNK
        bc = b_scr[r:r + GLA_CHUNK, :]
        qc = gq_scr[r:r + GLA_CHUNK, :]
        kc = gk_scr[r:r + GLA_CHUNK, :]
        vc = gv_scr[r:r + GLA_CHUNK, :]
        qs = [qc[m * SUBLANES:(m + 1) * SUBLANES] for m in range(nsub)]
        bs = [bc[m * SUBLANES:(m + 1) * SUBLANES] for m in range(nsub)]
        terms = []
        for j in range(GLA_CHUNK):
            kj = kc[j:j + 1, :]
            bj = bc[j:j + 1, :]
            for m in range(j // SUBLANES, nsub):
                diff = bs[m] - bj
                if m == j // SUBLANES and j % SUBLANES:
                    diff = jnp.where(sub_row >= j % SUBLANES, diff, -jnp.inf)
                terms.append(qs[m] * kj * jnp.exp2(diff))
        a_b = _dot(jnp.concatenate(terms, axis=0).astype(BF16), ind_scr[...])
        acc = [None] * nsub
        t = 0
        for j in range(GLA_CHUNK):
            vj = vc[j:j + 1, :]
            for m in range(j // SUBLANES, nsub):
                contrib = a_b[t * SUBLANES:(t + 1) * SUBLANES] * vj
                acc[m] = contrib if acc[m] is None else acc[m] + contrib
                t += 1
        o_intra = jnp.concatenate(acc, axis=0)
        b_last = bc[GLA_CHUNK - 1:GLA_CHUNK, :]
        st = st_scr[...]
        o_inter = _dot_nt((qc * jnp.exp2(bc)).astype(BF16), st.astype(BF16))
        kd = (kc * jnp.exp2(b_last - bc)).astype(BF16)
        kv = _dot_tn(vc.astype(BF16), kd)
        st_scr[...] = jnp.exp2(b_last) * st + kv * smask_scr[...]
        go_scr[r:r + GLA_CHUNK, :] = o_intra + o_inter

    issue_pieces(len(pieces))
    on = _headnorm(go_scr[...], e_scr[0:GLA_V_W, 0:GLA_V_W]) * og_ref[...]
    y_scr[:, SWA_Q_W:SWA_Q_W + GLA_V_W] = (on * _silu(gate_scr[...])).astype(BF16)

    yprev_scr[...] = y_scr[...]


def _mix(x2d, mem, mix_g, mem_g, w_in_r, w_mem, w_out, qg, kg, mqg, mkg, og, wgu, bg,
         sinks, relb, bkt, *, seq):
    tq = MIX_TQ
    ntiles = x2d.shape[0] // tq
    tiles_per_seq = seq // tq
    last = ntiles - 1
    smem = pl.BlockSpec(memory_space=pltpu.SMEM)
    in_specs = [
        smem, smem,
        pl.BlockSpec((tq, D_MODEL), lambda s: (jnp.maximum(s - 1, 0), 0)),
        pl.BlockSpec((tq, D_MODEL), lambda s: (jnp.minimum(s, last), 0)),
        pl.BlockSpec((tq, D_MODEL), lambda s: (jnp.minimum(s + 1, last), 0)),
        pl.BlockSpec((None, MEM_LEN, D_MODEL), lambda s: (jnp.minimum(s, last) // tiles_per_seq, 0, 0)),
        _const_spec((1, D_MODEL)), _const_spec((1, D_MODEL)),
        _const_spec((D_MODEL, IN_W_PAD)),
        _const_spec((D_MODEL, 2 * MEM_Q_W)),
        _const_spec((MIX_W, D_MODEL)),
        _const_spec((1, SWA_Q_W)), _const_spec((1, SWA_KV_W)),
        _const_spec((1, MEM_Q_W)), _const_spec((1, MEM_Q_W)), _const_spec((1, GLA_V_W)),
        _const_spec((LANES, GLA_QK_W)), _const_spec((1, GLA_QK_W)),
        _const_spec((BLOCK, 2 * BLOCK)),
    ]
    scratch = [
        pltpu.VMEM((SWA_HEADS, BLOCK, 2 * BLOCK), F32),
        pltpu.VMEM((E_SIZE, E_SIZE), BF16),
        pltpu.VMEM((tq, tq), BF16),
        pltpu.VMEM((GLA_QK_W, GLA_V_W), BF16),
        pltpu.VMEM((GLA_V_W, GLA_QK_W), F32),
        pltpu.VMEM((tq, IN_W_PAD), F32),
        pltpu.VMEM((BLOCK + tq, SWA_KV_W), F32),
        pltpu.VMEM((BLOCK + tq, SWA_KV_W), F32),
        pltpu.VMEM((MEM_LEN, MEM_Q_W), BF16),
        pltpu.VMEM((MEM_LEN, MEM_Q_W), BF16),
        pltpu.VMEM((GLA_V_W, GLA_QK_W), F32),
        pltpu.VMEM((tq, MIX_W), BF16),
        pltpu.VMEM((tq, MIX_W), BF16),
        pltpu.VMEM((tq, D_MODEL), BF16),
        pltpu.VMEM((tq, GLA_QK_W), F32),
        pltpu.VMEM((tq, GLA_QK_W), F32),
        pltpu.VMEM((tq, GLA_V_W), F32),
        pltpu.VMEM((tq, GLA_V_W), F32),
        pltpu.VMEM((tq, GLA_QK_W), F32),
        pltpu.VMEM((tq, GLA_V_W), F32),
    ]
    return pl.pallas_call(
        functools.partial(_mix_kernel, tiles_per_seq=tiles_per_seq),
        grid=(ntiles + 1,),
        in_specs=in_specs,
        out_specs=pl.BlockSpec((tq, D_MODEL), lambda s: (jnp.maximum(s - 1, 0), 0)),
        out_shape=jax.ShapeDtypeStruct(x2d.shape, F32),
        scratch_shapes=scratch,
        compiler_params=pltpu.CompilerParams(
            dimension_semantics=("arbitrary",), vmem_limit_bytes=VMEM_LIMIT),
        name="mixing",
    )(sinks, relb, x2d, x2d, x2d, mem, mix_g, mem_g, w_in_r, w_mem, w_out, qg, kg, mqg, mkg, og,
      wgu, bg, bkt)


def _tile_gain(g, reps):
    return jnp.tile(g.astype(F32), reps).reshape(1, -1)


def kernel(x, mem, ffn1_norm, ffn1_w_gate, ffn1_w_up, ffn1_w_down, mix_norm, mem_norm, w_in, w_mem_kv, swa_q_norm, swa_k_norm, swa_sinks, rel_bias, gla_w_gate_up, gla_b_gate, gla_out_norm, mem_q_norm, mem_k_norm, w_out, ffn2_norm, ffn2_w_gate, ffn2_w_up, ffn2_w_down):
    bsz, seq, d = x.shape
    depth = ffn1_norm.shape[0]
    bkt = jnp.asarray(_t5_bucket_table())
    relb = rel_bias.astype(F32).reshape(-1)
    x2d = x.reshape(bsz * seq, d)
    for l in range(depth):
        x2d, (wg2, wu2, wd2, w_out_b, w_mem_b, w_in_r) = _ffn(
            x2d, ffn1_norm[l].reshape(1, d),
            ffn1_w_gate[l].astype(BF16), ffn1_w_up[l].astype(BF16), ffn1_w_down[l].astype(BF16),
            cast=(ffn2_w_gate[l], ffn2_w_up[l], ffn2_w_down[l], w_out[l], w_mem_kv[l]), w_in=w_in[l])
        wgu = jnp.concatenate(
            [gla_w_gate_up[l], jnp.zeros((LANES - GLA_RANK, GLA_QK_W), gla_w_gate_up.dtype)],
            axis=0).astype(BF16)
        x2d = _mix(x2d, mem, mix_norm[l].reshape(1, d), mem_norm[l].reshape(1, d),
                   w_in_r, w_mem_b, w_out_b,
                   _tile_gain(swa_q_norm[l], SWA_HEADS), _tile_gain(swa_k_norm[l], SWA_KV_HEADS),
                   _tile_gain(mem_q_norm[l], MEM_HEADS), _tile_gain(mem_k_norm[l], MEM_HEADS),
                   _tile_gain(gla_out_norm[l], GLA_HEADS), wgu, gla_b_gate[l].reshape(1, -1),
                   swa_sinks[l].astype(F32), relb, bkt, seq=seq)
        x2d, _ = _ffn(x2d, ffn2_norm[l].reshape(1, d), wg2, wu2, wd2)
    return x2d.reshape(bsz, seq, d)
```

```python
import functools
import math

import numpy as np
import jax
import jax.numpy as jnp
from jax import lax
from jax.experimental import pallas as pl
from jax.experimental.pallas import tpu as pltpu

F32 = jnp.float32
BF16 = jnp.bfloat16

D_MODEL = 1024
MEM_LEN = 256
HEAD_DIM = 64
SWA_HEADS = 8
SWA_KV_HEADS = 2
SWA_GROUP = SWA_HEADS // SWA_KV_HEADS
WINDOW = 128
BLOCK = 128
N_BUCKETS = 32
MAX_DISTANCE = 128
GLA_HEADS = 4
GLA_DK = 32
GLA_DV = 64
GLA_RANK = 16
GLA_TAU = 16.0
GLA_CHUNK = 32
MEM_HEADS = 4
D_FF = 2816
EPS = 1e-6
LOG2E = math.log2(math.e)

SWA_Q_W = SWA_HEADS * HEAD_DIM
SWA_KV_W = SWA_KV_HEADS * HEAD_DIM
GLA_QK_W = GLA_HEADS * GLA_DK
GLA_V_W = GLA_HEADS * GLA_DV
MEM_Q_W = MEM_HEADS * HEAD_DIM
MIX_W = SWA_Q_W + GLA_V_W + MEM_Q_W

LANES = 128
SUBLANES = 8
BF16_ROWS = 16

C_SQ = 0
C_SK = C_SQ + SWA_Q_W
C_SV = C_SK + SWA_KV_W
C_GQ = C_SV + SWA_KV_W
C_GK = C_GQ + GLA_QK_W
C_GV = C_GK + GLA_QK_W
C_GG = C_GV + GLA_V_W
C_MQ = C_GG + GLA_V_W
C_LR = C_MQ + MEM_Q_W
IN_W_PAD = C_LR + LANES

FFN_TM = 1024
MIX_TQ = 512
MXU_W = 256
PIECE_W = MXU_W
VMEM_LIMIT = 56 * 1024 * 1024


def _t5_bucket_table():
    qi = np.arange(BLOCK)[:, None]
    kj = np.arange(2 * BLOCK)[None, :]
    dist = np.maximum(qi + BLOCK - kj, 0)
    max_exact = N_BUCKETS // 2
    d = np.maximum(dist, 1).astype(np.float64)
    val = np.log(d / max_exact) / math.log(MAX_DISTANCE / max_exact) * (N_BUCKETS - max_exact)
    in_win = (qi + BLOCK - kj >= 0) & (qi + BLOCK - kj < WINDOW)
    frac = np.abs(val - np.round(val))
    assert np.all((frac > 1e-6) | (d <= max_exact) | ~in_win)
    large = np.minimum(max_exact + np.floor(val).astype(np.int64), N_BUCKETS - 1)
    bkt = np.where(dist < max_exact, dist, large)
    bkt = np.where(in_win, bkt, -1)
    return bkt.astype(np.int32)


def _rmsnorm(x, g):
    ms = jnp.mean(x * x, axis=-1, keepdims=True)
    return x * lax.rsqrt(ms + EPS) * g


def _dot(a, b):
    return jnp.dot(a, b, preferred_element_type=F32)


def _dot_nt(a, b):
    return lax.dot_general(a, b, (((1,), (1,)), ((), ())), preferred_element_type=F32)


def _dot_tn(a, b):
    return lax.dot_general(a, b, (((0,), (0,)), ((), ())), preferred_element_type=F32)


def _headnorm(x, e_ref):
    x2 = (x * x).astype(BF16)
    w = x.shape[1]
    if w <= MXU_W:
        ms = _dot(x2, e_ref[0:w, 0:w])
    else:
        ms = jnp.concatenate([_dot(x2[:, c:c + MXU_W], e_ref[...]) for c in range(0, w, MXU_W)], axis=1)
    return x * lax.rsqrt(ms + EPS)


def _shift_div(x, d):
    assert d & (d - 1) == 0
    return x >> (d.bit_length() - 1)


def _silu(x):
    return x * (1.0 / (1.0 + jnp.exp(-x)))


def _ffn_kernel(x_ref, g_ref, wg_ref, wu_ref, wd_ref, *rest, n_cast, has_w_in):
    n_in = n_cast + int(has_w_in)
    cast_in, o_ref, cast_out = rest[:n_in], rest[n_in], rest[n_in + 1:]
    x = x_ref[...]
    h = _rmsnorm(x, g_ref[...]).astype(BF16)
    g = _dot(h, wg_ref[...])
    u = _dot(h, wu_ref[...])
    y = _dot((_silu(g) * u).astype(BF16), wd_ref[...])
    o_ref[...] = x + 0.5 * y
    for src, dst in zip(cast_in[:n_cast], cast_out[:n_cast]):
        dst[...] = src[...].astype(BF16)
    if has_w_in:
        src, dst = cast_in[n_cast], cast_out[n_cast]
        w = src[...]
        dst[:, 0:C_MQ] = w[:, 0:C_MQ].astype(BF16)
        dst[:, C_MQ:C_LR] = w[:, C_MQ + GLA_RANK:C_MQ + GLA_RANK + MEM_Q_W].astype(BF16)
        slab = w[:, C_MQ:C_MQ + LANES]
        lane = lax.broadcasted_iota(jnp.int32, slab.shape, 1)
        dst[:, C_LR:C_LR + LANES] = jnp.where(lane < GLA_RANK, slab, 0.0).astype(BF16)


def _const_spec(shape):
    return pl.BlockSpec(shape, lambda *_: (0,) * len(shape), pipeline_mode=pl.Buffered(1))


def _ffn(x2d, gain, wg, wu, wd, cast=(), w_in=None):
    t = x2d.shape[0]
    steps = t // FFN_TM
    extra_in, extra_specs, extra_out_specs, extra_out_shapes = [], [], [], []
    for w in tuple(cast) + ((w_in,) if w_in is not None else ()):
        rows, cols = w.shape
        nblk = steps
        while rows % nblk or (rows // nblk) % BF16_ROWS:
            nblk //= 2
        blk = rows // nblk
        out_cols = IN_W_PAD if w is w_in else cols
        extra_in.append(w)
        index_map = functools.partial(lambda i, n: (jnp.minimum(i, n - 1), 0), n=nblk)
        extra_specs.append(pl.BlockSpec((blk, cols), index_map))
        extra_out_specs.append(pl.BlockSpec((blk, out_cols), index_map))
        extra_out_shapes.append(jax.ShapeDtypeStruct((rows, out_cols), BF16))
    outs = pl.pallas_call(
        functools.partial(_ffn_kernel, n_cast=len(cast), has_w_in=w_in is not None),
        grid=(steps,),
        in_specs=[
            pl.BlockSpec((FFN_TM, D_MODEL), lambda i: (i, 0)),
            _const_spec((1, D_MODEL)),
            _const_spec((D_MODEL, D_FF)),
            _const_spec((D_MODEL, D_FF)),
            _const_spec((D_FF, D_MODEL)),
        ] + extra_specs,
        out_specs=[pl.BlockSpec((FFN_TM, D_MODEL), lambda i: (i, 0))] + extra_out_specs,
        out_shape=[jax.ShapeDtypeStruct((t, D_MODEL), F32)] + extra_out_shapes,
        compiler_params=pltpu.CompilerParams(
            dimension_semantics=("arbitrary",), vmem_limit_bytes=VMEM_LIMIT),
        name="ffn_halfstep",
    )(x2d, gain, wg, wu, wd, *extra_in)
    return outs[0], outs[1:]


def _mix_kernel(sinks_ref, relb_ref,
                xp_ref, xc_ref, xn_ref, mem_ref, mixg_ref, memg_ref, win_ref, wmem_ref, wout_ref,
                qg_ref, kg_ref, mqg_ref, mkg_ref, og_ref, wgu_ref, bg_ref, bkt_ref,
                o_ref,
                bias_scr, e_scr, tri_scr, ind_scr, smask_scr,
                proj_scr, kext_scr, vext_scr, mk_scr, mv_scr, st_scr, y_scr, yprev_scr,
                h_scr, gq_scr, gk_scr, gv_scr, gate_scr, b_scr, go_scr,
                *, tiles_per_seq):
    step = pl.program_id(0)
    first_of_seq = lax.rem(step, tiles_per_seq) == 0
    tq = MIX_TQ
    nblk = tq // BLOCK
    nchunk = tq // GLA_CHUNK

    @pl.when(step == 0)
    def _prologue():
        bkt = bkt_ref[...]
        for h in range(SWA_HEADS):
            acc = jnp.zeros((BLOCK, 2 * BLOCK), F32)
            for k in range(N_BUCKETS):
                acc = jnp.where(bkt == k, relb_ref[k * SWA_HEADS + h] * LOG2E, acc)
            bias_scr[h] = jnp.where(bkt >= 0, acc, -jnp.inf)
        r = lax.broadcasted_iota(jnp.int32, (MXU_W, MXU_W), 0)
        c = lax.broadcasted_iota(jnp.int32, (MXU_W, MXU_W), 1)
        same_head = _shift_div(r, HEAD_DIM) == _shift_div(c, HEAD_DIM)
        e_scr[...] = jnp.where(same_head, 1.0 / HEAD_DIM, 0.0).astype(BF16)
        same_chunk = _shift_div(r, GLA_CHUNK) == _shift_div(c, GLA_CHUNK)
        tri_scr[...] = jnp.where(same_chunk & (c <= r), 1.0, 0.0).astype(BF16)
        r = lax.broadcasted_iota(jnp.int32, (GLA_QK_W, GLA_V_W), 0)
        c = lax.broadcasted_iota(jnp.int32, (GLA_QK_W, GLA_V_W), 1)
        same_head = _shift_div(r, GLA_DK) == _shift_div(c, GLA_DV)
        ind_scr[...] = jnp.where(same_head, 1.0, 0.0).astype(BF16)
        smask_scr[...] = jnp.where(same_head, 1.0, 0.0).astype(F32)
        yprev_scr[...] = jnp.zeros(yprev_scr.shape, BF16)
        h0 = _rmsnorm(xc_ref[...], mixg_ref[...]).astype(BF16)
        proj_scr[...] = _dot(h0, win_ref[...])

    @pl.when(first_of_seq)
    def _init_sequence():
        kext_scr[0:BLOCK, :] = jnp.zeros((BLOCK, SWA_KV_W), F32)
        vext_scr[0:BLOCK, :] = jnp.zeros((BLOCK, SWA_KV_W), F32)
        st_scr[...] = jnp.zeros(st_scr.shape, F32)
        hm = _rmsnorm(mem_ref[...], memg_ref[...]).astype(BF16)
        kv = _dot(hm, wmem_ref[...])
        mk = _headnorm(kv[:, :MEM_Q_W], e_scr) * mkg_ref[...]
        mk_scr[...] = mk.astype(BF16)
        mv_scr[...] = kv[:, MEM_Q_W:].astype(BF16)

    qn = _headnorm(proj_scr[:, C_SQ:C_SQ + SWA_Q_W], e_scr) * qg_ref[...]
    qn = (qn * (HEAD_DIM ** -0.5 * LOG2E)).astype(BF16)
    kn = _headnorm(proj_scr[:, C_SK:C_SK + SWA_KV_W], e_scr) * kg_ref[...]
    kext_scr[BLOCK:BLOCK + tq, :] = kn
    vext_scr[BLOCK:BLOCK + tq, :] = proj_scr[:, C_SV:C_SV + SWA_KV_W]
    mqn = _headnorm(proj_scr[:, C_MQ:C_MQ + MEM_Q_W], e_scr) * mqg_ref[...]
    mqn = (mqn * (HEAD_DIM ** -0.5 * LOG2E)).astype(BF16)
    z = _dot(proj_scr[:, C_LR:C_LR + LANES].astype(BF16), wgu_ref[...]) + bg_ref[...]
    log_a = (jnp.minimum(z, 0.0) - jnp.log(1.0 + jnp.exp(-jnp.abs(z)))) * (1.0 / GLA_TAU)
    la_hi = log_a.astype(BF16)
    la_lo = (log_a - la_hi.astype(F32)).astype(BF16)
    la2 = jnp.concatenate([la_hi, la_lo], axis=1)
    for r0 in range(0, tq, MXU_W):
        cs = _dot(tri_scr[...], la2[r0:r0 + MXU_W])
        b_scr[r0:r0 + MXU_W, :] = (cs[:, 0:LANES] + cs[:, LANES:2 * LANES]) * LOG2E
    gq_scr[...] = proj_scr[:, C_GQ:C_GQ + GLA_QK_W] * (GLA_DK ** -0.5)
    gk_scr[...] = proj_scr[:, C_GK:C_GK + GLA_QK_W]
    gv_scr[...] = proj_scr[:, C_GV:C_GV + GLA_V_W]
    gate_scr[...] = proj_scr[:, C_GG:C_GG + GLA_V_W]

    h_scr[...] = _rmsnorm(xn_ref[...], mixg_ref[...]).astype(BF16)

    def project_piece(k):
        c0, c1 = k * PIECE_W, min((k + 1) * PIECE_W, IN_W_PAD)
        proj_scr[:, c0:c1] = _dot(h_scr[...], win_ref[:, c0:c1])

    def output_piece(k):
        c0, c1 = k * PIECE_W, (k + 1) * PIECE_W
        o_ref[:, c0:c1] = xp_ref[:, c0:c1] + _dot(yprev_scr[...], wout_ref[:, c0:c1])

    n_proj = -(-IN_W_PAD // PIECE_W)
    pieces = [functools.partial(project_piece, k) for k in range(n_proj)]
    pieces += [functools.partial(output_piece, k) for k in range(D_MODEL // PIECE_W)]

    def issue_pieces(n):
        for _ in range(n):
            if pieces:
                pieces.pop(0)()

    lane_lo = lax.broadcasted_iota(jnp.int32, (BLOCK, LANES), 1) < HEAD_DIM
    k_a = kext_scr[...]
    v_a = vext_scr[...]
    k_b = pltpu.roll(k_a, HEAD_DIM, 1)
    v_b = pltpu.roll(v_a, HEAD_DIM, 1)
    ext_lo = lax.broadcasted_iota(jnp.int32, k_a.shape, 1) < HEAD_DIM
    kk = (jnp.where(ext_lo, k_a, k_b).astype(BF16), jnp.where(ext_lo, k_b, k_a).astype(BF16))
    vv = (jnp.where(ext_lo, v_a, v_b).astype(BF16), jnp.where(ext_lo, v_b, v_a).astype(BF16))
    col = lax.broadcasted_iota(jnp.int32, (BLOCK, 2 * BLOCK), 1)
    zero_q = jnp.zeros((BLOCK, LANES), BF16)

    for nb in range(nblk):
        r0 = nb * BLOCK
        for hk in range(SWA_KV_HEADS):
            kband = kk[hk][r0:r0 + 2 * BLOCK]
            vband = vv[hk][r0:r0 + 2 * BLOCK]
            qparts = []
            for pair in range(2):
                c0 = hk * SWA_GROUP * HEAD_DIM + pair * LANES
                qp = qn[r0:r0 + BLOCK, c0:c0 + LANES]
                qparts.append(jnp.where(lane_lo, qp, zero_q))
                qparts.append(jnp.where(lane_lo, zero_q, qp))
            s_all = _dot_nt(jnp.concatenate(qparts, axis=0), kband)
            ps, inv = [], []
            for g in range(SWA_GROUP):
                head = hk * SWA_GROUP + g
                s = s_all[g * BLOCK:(g + 1) * BLOCK] + bias_scr[head]
                if nb == 0:
                    s = jnp.where((col >= BLOCK) | jnp.logical_not(first_of_seq), s, -jnp.inf)
                sink = sinks_ref[head] * LOG2E
                m = jnp.maximum(jnp.max(s, axis=-1, keepdims=True), sink)
                p = jnp.exp2(s - m)
                den = jnp.sum(p, axis=-1, keepdims=True) + jnp.exp2(sink - m)
                ps.append(p.astype(BF16))
                inv.append(1.0 / den)
            o_all = _dot(jnp.concatenate(ps, axis=0), vband)
            for pair in range(2):
                g0 = 2 * pair
                o_pair = jnp.where(lane_lo, o_all[g0 * BLOCK:(g0 + 1) * BLOCK] * inv[g0],
                                   o_all[(g0 + 1) * BLOCK:(g0 + 2) * BLOCK] * inv[g0 + 1])
                c0 = hk * SWA_GROUP * HEAD_DIM + pair * LANES
                y_scr[r0:r0 + BLOCK, c0:c0 + LANES] = o_pair.astype(BF16)

    kext_scr[0:BLOCK, :] = kext_scr[tq:tq + BLOCK, :]
    vext_scr[0:BLOCK, :] = vext_scr[tq:tq + BLOCK, :]

    lane_lo_t = lax.broadcasted_iota(jnp.int32, (tq, LANES), 1) < HEAD_DIM
    zero_t = jnp.zeros((tq, LANES), BF16)
    for pair in range(MEM_HEADS // 2):
        issue_pieces(2)
        qp = mqn[:, pair * LANES:(pair + 1) * LANES]
        qst = jnp.concatenate([jnp.where(lane_lo_t, qp, zero_t), jnp.where(lane_lo_t, zero_t, qp)], axis=0)
        s = _dot_nt(qst, mk_scr[:, pair * LANES:(pair + 1) * LANES])
        m = jnp.max(s, axis=-1, keepdims=True)
        p = jnp.exp2(s - m)
        rinv = 1.0 / jnp.sum(p, axis=-1, keepdims=True)
        o2 = _dot(p.astype(BF16), mv_scr[:, pair * LANES:(pair + 1) * LANES]) * rinv
        o_pair = jnp.where(lane_lo_t, o2[:tq], o2[tq:])
        c0 = SWA_Q_W + GLA_V_W + pair * LANES
        y_scr[:, c0:c0 + LANES] = o_pair.astype(BF16)

    sub_row = lax.broadcasted_iota(jnp.int32, (SUBLANES, GLA_QK_W), 0)
    nsub = GLA_CHUNK // SUBLANES
    eye_qk = (lax.broadcasted_iota(jnp.int32, (GLA_QK_W, GLA_QK_W), 0)
              == lax.broadcasted_iota(jnp.int32, (GLA_QK_W, GLA_QK_W), 1))

    for c in range(nchunk):
        if c % 2 == 0:
            issue_pieces(1)
        r = c * GLA_CHUNK
        bc = b_scr[r:r + GLA_CHUNK, :]
        qc = gq_scr[r:r + GLA_CHUNK, :]
        kc = gk_scr[r:r + GLA_CHUNK, :]
        vc = gv_scr[r:r + GLA_CHUNK, :]
        qs = [qc[m * SUBLANES:(m + 1) * SUBLANES] for m in range(nsub)]
        bs = [bc[m * SUBLANES:(m + 1) * SUBLANES] for m in range(nsub)]
        terms = []
        for j in range(GLA_CHUNK):
            kj = kc[j:j + 1, :]
            bj = bc[j:j + 1, :]
            for m in range(j // SUBLANES, nsub):
                diff = bs[m] - bj
                if m == j // SUBLANES and j % SUBLANES:
                    diff = jnp.where(sub_row >= j % SUBLANES, diff, -jnp.inf)
                terms.append(qs[m] * kj * jnp.exp2(diff))
        a_b = _dot(jnp.concatenate(terms, axis=0).astype(BF16), ind_scr[...])
        acc = [None] * nsub
        t = 0
        for j in range(GLA_CHUNK):
            vj = vc[j:j + 1, :]
            for m in range(j // SUBLANES, nsub):
                contrib = a_b[t * SUBLANES:(t + 1) * SUBLANES] * vj
                acc[m] = contrib if acc[m] is None else acc[m] + contrib
                t += 1
        o_intra = jnp.concatenate(acc, axis=0)
        b_last = bc[GLA_CHUNK - 1:GLA_CHUNK, :]
        st = st_scr[...]
        o_inter = _dot((qc * jnp.exp2(bc)).astype(BF16), st.astype(BF16))
        kd = (kc * jnp.exp2(b_last - bc)).astype(BF16)
        kv = _dot_tn(kd, vc.astype(BF16))
        b_col = jnp.sum(jnp.where(eye_qk, b_last, 0.0), axis=1, keepdims=True)
        st_scr[...] = jnp.exp2(b_col) * st + kv * smask_scr[...]
        go_scr[r:r + GLA_CHUNK, :] = o_intra + o_inter

    issue_pieces(len(pieces))
    on = _headnorm(go_scr[...], e_scr) * og_ref[...]
    y_scr[:, SWA_Q_W:SWA_Q_W + GLA_V_W] = (on * _silu(gate_scr[...])).astype(BF16)

    yprev_scr[...] = y_scr[...]


def _mix(x2d, mem, mix_g, mem_g, w_in_r, w_mem, w_out, qg, kg, mqg, mkg, og, wgu, bg,
         sinks, relb, bkt, *, seq):
    tq = MIX_TQ
    ntiles = x2d.shape[0] // tq
    tiles_per_seq = seq // tq
    last = ntiles - 1
    smem = pl.BlockSpec(memory_space=pltpu.SMEM)
    in_specs = [
        smem, smem,
        pl.BlockSpec((tq, D_MODEL), lambda s: (jnp.maximum(s - 1, 0), 0)),
        pl.BlockSpec((tq, D_MODEL), lambda s: (jnp.minimum(s, last), 0)),
        pl.BlockSpec((tq, D_MODEL), lambda s: (jnp.minimum(s + 1, last), 0)),
        pl.BlockSpec((None, MEM_LEN, D_MODEL), lambda s: (jnp.minimum(s, last) // tiles_per_seq, 0, 0)),
        _const_spec((1, D_MODEL)), _const_spec((1, D_MODEL)),
        _const_spec((D_MODEL, IN_W_PAD)),
        _const_spec((D_MODEL, 2 * MEM_Q_W)),
        _const_spec((MIX_W, D_MODEL)),
        _const_spec((1, SWA_Q_W)), _const_spec((1, SWA_KV_W)),
        _const_spec((1, MEM_Q_W)), _const_spec((1, MEM_Q_W)), _const_spec((1, GLA_V_W)),
        _const_spec((LANES, GLA_QK_W)), _const_spec((1, GLA_QK_W)),
        _const_spec((BLOCK, 2 * BLOCK)),
    ]
    scratch = [
        pltpu.VMEM((SWA_HEADS, BLOCK, 2 * BLOCK), F32),
        pltpu.VMEM((MXU_W, MXU_W), BF16),
        pltpu.VMEM((MXU_W, MXU_W), BF16),
        pltpu.VMEM((GLA_QK_W, GLA_V_W), BF16),
        pltpu.VMEM((GLA_QK_W, GLA_V_W), F32),
        pltpu.VMEM((tq, IN_W_PAD), F32),
        pltpu.VMEM((BLOCK + tq, SWA_KV_W), F32),
        pltpu.VMEM((BLOCK + tq, SWA_KV_W), F32),
        pltpu.VMEM((MEM_LEN, MEM_Q_W), BF16),
        pltpu.VMEM((MEM_LEN, MEM_Q_W), BF16),
        pltpu.VMEM((GLA_QK_W, GLA_V_W), F32),
        pltpu.VMEM((tq, MIX_W), BF16),
        pltpu.VMEM((tq, MIX_W), BF16),
        pltpu.VMEM((tq, D_MODEL), BF16),
        pltpu.VMEM((tq, GLA_QK_W), F32),
        pltpu.VMEM((tq, GLA_QK_W), F32),
        pltpu.VMEM((tq, GLA_V_W), F32),
        pltpu.VMEM((tq, GLA_V_W), F32),
        pltpu.VMEM((tq, GLA_QK_W), F32),
        pltpu.VMEM((tq, GLA_V_W), F32),
    ]
    return pl.pallas_call(
        functools.partial(_mix_kernel, tiles_per_seq=tiles_per_seq),
        grid=(ntiles + 1,),
        in_specs=in_specs,
        out_specs=pl.BlockSpec((tq, D_MODEL), lambda s: (jnp.maximum(s - 1, 0), 0)),
        out_shape=jax.ShapeDtypeStruct(x2d.shape, F32),
        scratch_shapes=scratch,
        compiler_params=pltpu.CompilerParams(
            dimension_semantics=("arbitrary",), vmem_limit_bytes=VMEM_LIMIT),
        name="mixing",
    )(sinks, relb, x2d, x2d, x2d, mem, mix_g, mem_g, w_in_r, w_mem, w_out, qg, kg, mqg, mkg, og,
      wgu, bg, bkt)


def _tile_gain(g, reps):
    return jnp.tile(g.astype(F32), reps).reshape(1, -1)


def kernel(x, mem, ffn1_norm, ffn1_w_gate, ffn1_w_up, ffn1_w_down, mix_norm, mem_norm, w_in, w_mem_kv, swa_q_norm, swa_k_norm, swa_sinks, rel_bias, gla_w_gate_up, gla_b_gate, gla_out_norm, mem_q_norm, mem_k_norm, w_out, ffn2_norm, ffn2_w_gate, ffn2_w_up, ffn2_w_down):
    bsz, seq, d = x.shape
    depth = ffn1_norm.shape[0]
    bkt = jnp.asarray(_t5_bucket_table())
    relb = rel_bias.astype(F32).reshape(-1)
    x2d = x.reshape(bsz * seq, d)
    for l in range(depth):
        x2d, (wg2, wu2, wd2, w_out_b, w_mem_b, w_in_r) = _ffn(
            x2d, ffn1_norm[l].reshape(1, d),
            ffn1_w_gate[l].astype(BF16), ffn1_w_up[l].astype(BF16), ffn1_w_down[l].astype(BF16),
            cast=(ffn2_w_gate[l], ffn2_w_up[l], ffn2_w_down[l], w_out[l], w_mem_kv[l]), w_in=w_in[l])
        wgu = jnp.concatenate(
            [gla_w_gate_up[l], jnp.zeros((LANES - GLA_RANK, GLA_QK_W), gla_w_gate_up.dtype)],
            axis=0).astype(BF16)
        x2d = _mix(x2d, mem, mix_norm[l].reshape(1, d), mem_norm[l].reshape(1, d),
                   w_in_r, w_mem_b, w_out_b,
                   _tile_gain(swa_q_norm[l], SWA_HEADS), _tile_gain(swa_k_norm[l], SWA_KV_HEADS),
                   _tile_gain(mem_q_norm[l], MEM_HEADS), _tile_gain(mem_k_norm[l], MEM_HEADS),
                   _tile_gain(gla_out_norm[l], GLA_HEADS), wgu, gla_b_gate[l].reshape(1, -1),
                   swa_sinks[l].astype(F32), relb, bkt, seq=seq)
        x2d, _ = _ffn(x2d, ffn2_norm[l].reshape(1, d), wg2, wu2, wd2)
    return x2d.reshape(bsz, seq, d)
```

```python
import functools
import math

import numpy as np
import jax
import jax.numpy as jnp
from jax import lax
from jax.experimental import pallas as pl
from jax.experimental.pallas import tpu as pltpu

F32 = jnp.float32
BF16 = jnp.bfloat16

D_MODEL = 1024
MEM_LEN = 256
HEAD_DIM = 64
SWA_HEADS = 8
SWA_KV_HEADS = 2
SWA_GROUP = SWA_HEADS // SWA_KV_HEADS
WINDOW = 128
BLOCK = 128
N_BUCKETS = 32
MAX_DISTANCE = 128
GLA_HEADS = 4
GLA_DK = 32
GLA_DV = 64
GLA_RANK = 16
GLA_TAU = 16.0
GLA_CHUNK = 32
MEM_HEADS = 4
D_FF = 2816
EPS = 1e-6
LOG2E = math.log2(math.e)

SWA_Q_W = SWA_HEADS * HEAD_DIM
SWA_KV_W = SWA_KV_HEADS * HEAD_DIM
GLA_QK_W = GLA_HEADS * GLA_DK
GLA_V_W = GLA_HEADS * GLA_DV
MEM_Q_W = MEM_HEADS * HEAD_DIM
MIX_W = SWA_Q_W + GLA_V_W + MEM_Q_W

LANES = 128
SUBLANES = 8
BF16_ROWS = 16

C_SQ = 0
C_SK = C_SQ + SWA_Q_W
C_SV = C_SK + SWA_KV_W
C_GQ = C_SV + SWA_KV_W
C_GK = C_GQ + GLA_QK_W
C_GV = C_GK + GLA_QK_W
C_GG = C_GV + GLA_V_W
C_MQ = C_GG + GLA_V_W
C_LR = C_MQ + MEM_Q_W
IN_W_PAD = C_LR + LANES

FFN_TM = 1024
MIX_TQ = 512
MXU_W = 256
PIECE_W = MXU_W
VMEM_LIMIT = 56 * 1024 * 1024


def _t5_bucket_table():
    qi = np.arange(BLOCK)[:, None]
    kj = np.arange(2 * BLOCK)[None, :]
    dist = np.maximum(qi + BLOCK - kj, 0)
    max_exact = N_BUCKETS // 2
    d = np.maximum(dist, 1).astype(np.float64)
    val = np.log(d / max_exact) / math.log(MAX_DISTANCE / max_exact) * (N_BUCKETS - max_exact)
    in_win = (qi + BLOCK - kj >= 0) & (qi + BLOCK - kj < WINDOW)
    frac = np.abs(val - np.round(val))
    assert np.all((frac > 1e-6) | (d <= max_exact) | ~in_win)
    large = np.minimum(max_exact + np.floor(val).astype(np.int64), N_BUCKETS - 1)
    bkt = np.where(dist < max_exact, dist, large)
    bkt = np.where(in_win, bkt, -1)
    return bkt.astype(np.int32)


def _rmsnorm(x, g):
    ms = jnp.mean(x * x, axis=-1, keepdims=True)
    return x * lax.rsqrt(ms + EPS) * g


def _dot(a, b):
    return jnp.dot(a, b, preferred_element_type=F32)


def _dot_nt(a, b):
    return lax.dot_general(a, b, (((1,), (1,)), ((), ())), preferred_element_type=F32)


def _dot_tn(a, b):
    return lax.dot_general(a, b, (((0,), (0,)), ((), ())), preferred_element_type=F32)


def _headnorm(x, e_ref):
    x2 = (x * x).astype(BF16)
    w = x.shape[1]
    if w <= MXU_W:
        ms = _dot(x2, e_ref[0:w, 0:w])
    else:
        ms = jnp.concatenate([_dot(x2[:, c:c + MXU_W], e_ref[...]) for c in range(0, w, MXU_W)], axis=1)
    return x * lax.rsqrt(ms + EPS)


def _shift_div(x, d):
    assert d & (d - 1) == 0
    return x >> (d.bit_length() - 1)


def _silu(x):
    return x * (1.0 / (1.0 + jnp.exp(-x)))


def _ffn_kernel(x_ref, g_ref, wg_ref, wu_ref, wd_ref, *rest, n_cast, has_w_in):
    n_in = n_cast + int(has_w_in)
    cast_in, o_ref, cast_out = rest[:n_in], rest[n_in], rest[n_in + 1:]
    x = x_ref[...]
    h = _rmsnorm(x, g_ref[...]).astype(BF16)
    g = _dot(h, wg_ref[...])
    u = _dot(h, wu_ref[...])
    y = _dot((_silu(g) * u).astype(BF16), wd_ref[...])
    o_ref[...] = x + 0.5 * y
    for src, dst in zip(cast_in[:n_cast], cast_out[:n_cast]):
        dst[...] = src[...].astype(BF16)
    if has_w_in:
        src, dst = cast_in[n_cast], cast_out[n_cast]
        w = src[...]
        dst[:, 0:C_MQ] = w[:, 0:C_MQ].astype(BF16)
        dst[:, C_MQ:C_LR] = w[:, C_MQ + GLA_RANK:C_MQ + GLA_RANK + MEM_Q_W].astype(BF16)
        slab = w[:, C_MQ:C_MQ + LANES]
        lane = lax.broadcasted_iota(jnp.int32, slab.shape, 1)
        dst[:, C_LR:C_LR + LANES] = jnp.where(lane < GLA_RANK, slab, 0.0).astype(BF16)


def _const_spec(shape):
    return pl.BlockSpec(shape, lambda *_: (0,) * len(shape), pipeline_mode=pl.Buffered(1))


def _ffn(x2d, gain, wg, wu, wd, cast=(), w_in=None):
    t = x2d.shape[0]
    steps = t // FFN_TM
    extra_in, extra_specs, extra_out_specs, extra_out_shapes = [], [], [], []
    for w in tuple(cast) + ((w_in,) if w_in is not None else ()):
        rows, cols = w.shape
        nblk = steps
        while rows % nblk or (rows // nblk) % BF16_ROWS:
            nblk //= 2
        blk = rows // nblk
        out_cols = IN_W_PAD if w is w_in else cols
        extra_in.append(w)
        index_map = functools.partial(lambda i, n: (jnp.minimum(i, n - 1), 0), n=nblk)
        extra_specs.append(pl.BlockSpec((blk, cols), index_map))
        extra_out_specs.append(pl.BlockSpec((blk, out_cols), index_map))
        extra_out_shapes.append(jax.ShapeDtypeStruct((rows, out_cols), BF16))
    outs = pl.pallas_call(
        functools.partial(_ffn_kernel, n_cast=len(cast), has_w_in=w_in is not None),
        grid=(steps,),
        in_specs=[
            pl.BlockSpec((FFN_TM, D_MODEL), lambda i: (i, 0)),
            _const_spec((1, D_MODEL)),
            _const_spec((D_MODEL, D_FF)),
            _const_spec((D_MODEL, D_FF)),
            _const_spec((D_FF, D_MODEL)),
        ] + extra_specs,
        out_specs=[pl.BlockSpec((FFN_TM, D_MODEL), lambda i: (i, 0))] + extra_out_specs,
        out_shape=[jax.ShapeDtypeStruct((t, D_MODEL), F32)] + extra_out_shapes,
        compiler_params=pltpu.CompilerParams(
            dimension_semantics=("arbitrary",), vmem_limit_bytes=VMEM_LIMIT),
        name="ffn_halfstep",
    )(x2d, gain, wg, wu, wd, *extra_in)
    return outs[0], outs[1:]


def _mix_kernel(*refs, tiles_per_seq, n_tiles):
    step = pl.program_id(0)

    @pl.when(step < n_tiles)
    def _steady():
        _mix_step(*refs, tiles_per_seq=tiles_per_seq)

    @pl.when(step == n_tiles)
    def _drain():
        r = dict(zip(_mix_step.__code__.co_varnames, refs))
        r["o_ref"][...] = r["xp_ref"][...] + _dot(r["yprev_scr"][...], r["wout_ref"][...])


def _mix_step(sinks_ref, relb_ref,
              xp_ref, xc_ref, xn_ref, mem_ref, mixg_ref, memg_ref, win_ref, wmem_ref, wout_ref,
              qg_ref, kg_ref, mqg_ref, mkg_ref, og_ref, wgu_ref, bg_ref, bkt_ref,
              o_ref,
              bias_scr, e_scr, tri_scr, ind_scr, smask_scr,
              proj_scr, kext_scr, vext_scr, mk_scr, mv_scr, st_scr, y_scr, yprev_scr,
              h_scr, gq_scr, gk_scr, gv_scr, gate_scr, b_scr, go_scr,
              *, tiles_per_seq):
    step = pl.program_id(0)
    first_of_seq = lax.rem(step, tiles_per_seq) == 0
    tq = MIX_TQ
    nblk = tq // BLOCK
    nchunk = tq // GLA_CHUNK

    @pl.when(step == 0)
    def _prologue():
        bkt = bkt_ref[...]
        for h in range(SWA_HEADS):
            acc = jnp.zeros((BLOCK, 2 * BLOCK), F32)
            for k in range(N_BUCKETS):
                acc = jnp.where(bkt == k, relb_ref[k * SWA_HEADS + h] * LOG2E, acc)
            bias_scr[h] = jnp.where(bkt >= 0, acc, -jnp.inf)
        r = lax.broadcasted_iota(jnp.int32, (MXU_W, MXU_W), 0)
        c = lax.broadcasted_iota(jnp.int32, (MXU_W, MXU_W), 1)
        same_head = _shift_div(r, HEAD_DIM) == _shift_div(c, HEAD_DIM)
        e_scr[...] = jnp.where(same_head, 1.0 / HEAD_DIM, 0.0).astype(BF16)
        same_chunk = _shift_div(r, GLA_CHUNK) == _shift_div(c, GLA_CHUNK)
        tri_scr[...] = jnp.where(same_chunk & (c <= r), 1.0, 0.0).astype(BF16)
        r = lax.broadcasted_iota(jnp.int32, (GLA_QK_W, GLA_V_W), 0)
        c = lax.broadcasted_iota(jnp.int32, (GLA_QK_W, GLA_V_W), 1)
        same_head = _shift_div(r, GLA_DK) == _shift_div(c, GLA_DV)
        ind_scr[...] = jnp.where(same_head, 1.0, 0.0).astype(BF16)
        smask_scr[...] = jnp.where(same_head, 1.0, 0.0).astype(F32)
        yprev_scr[...] = jnp.zeros(yprev_scr.shape, BF16)
        h0 = _rmsnorm(xc_ref[...], mixg_ref[...]).astype(BF16)
        proj_scr[...] = _dot(h0, win_ref[...])

    @pl.when(first_of_seq)
    def _init_sequence():
        kext_scr[0:BLOCK, :] = jnp.zeros((BLOCK, SWA_KV_W), F32)
        vext_scr[0:BLOCK, :] = jnp.zeros((BLOCK, SWA_KV_W), F32)
        st_scr[...] = jnp.zeros(st_scr.shape, F32)
        hm = _rmsnorm(mem_ref[...], memg_ref[...]).astype(BF16)
        kv = _dot(hm, wmem_ref[...])
        mk = _headnorm(kv[:, :MEM_Q_W], e_scr) * mkg_ref[...]
        mk_scr[...] = mk.astype(BF16)
        mv_scr[...] = kv[:, MEM_Q_W:].astype(BF16)

    qn = _headnorm(proj_scr[:, C_SQ:C_SQ + SWA_Q_W], e_scr) * qg_ref[...]
    qn = (qn * (HEAD_DIM ** -0.5 * LOG2E)).astype(BF16)
    kn = _headnorm(proj_scr[:, C_SK:C_SK + SWA_KV_W], e_scr) * kg_ref[...]
    kext_scr[BLOCK:BLOCK + tq, :] = kn
    vext_scr[BLOCK:BLOCK + tq, :] = proj_scr[:, C_SV:C_SV + SWA_KV_W]
    mqn = _headnorm(proj_scr[:, C_MQ:C_MQ + MEM_Q_W], e_scr) * mqg_ref[...]
    mqn = (mqn * (HEAD_DIM ** -0.5 * LOG2E)).astype(BF16)
    z = _dot(proj_scr[:, C_LR:C_LR + LANES].astype(BF16), wgu_ref[...]) + bg_ref[...]
    log_a = (jnp.minimum(z, 0.0) - jnp.log(1.0 + jnp.exp(-jnp.abs(z)))) * (1.0 / GLA_TAU)
    la_hi = log_a.astype(BF16)
    la_lo = (log_a - la_hi.astype(F32)).astype(BF16)
    la2 = jnp.concatenate([la_hi, la_lo], axis=1)
    for r0 in range(0, tq, MXU_W):
        cs = _dot(tri_scr[...], la2[r0:r0 + MXU_W])
        b_scr[r0:r0 + MXU_W, :] = (cs[:, 0:LANES] + cs[:, LANES:2 * LANES]) * LOG2E
    gq_scr[...] = proj_scr[:, C_GQ:C_GQ + GLA_QK_W] * (GLA_DK ** -0.5)
    gk_scr[...] = proj_scr[:, C_GK:C_GK + GLA_QK_W]
    gv_scr[...] = proj_scr[:, C_GV:C_GV + GLA_V_W]
    gate_scr[...] = proj_scr[:, C_GG:C_GG + GLA_V_W]

    h_scr[...] = _rmsnorm(xn_ref[...], mixg_ref[...]).astype(BF16)

    def project_piece(k):
        c0, c1 = k * PIECE_W, min((k + 1) * PIECE_W, IN_W_PAD)
        proj_scr[:, c0:c1] = _dot(h_scr[...], win_ref[:, c0:c1])

    def output_piece(k):
        c0, c1 = k * PIECE_W, (k + 1) * PIECE_W
        o_ref[:, c0:c1] = xp_ref[:, c0:c1] + _dot(yprev_scr[...], wout_ref[:, c0:c1])

    n_proj = -(-IN_W_PAD // PIECE_W)
    pieces = [functools.partial(project_piece, k) for k in range(n_proj)]
    pieces += [functools.partial(output_piece, k) for k in range(D_MODEL // PIECE_W)]

    def issue_pieces(n):
        for _ in range(n):
            if pieces:
                pieces.pop(0)()

    lane_lo = lax.broadcasted_iota(jnp.int32, (BLOCK, LANES), 1) < HEAD_DIM
    k_a = kext_scr[...]
    v_a = vext_scr[...]
    k_b = pltpu.roll(k_a, HEAD_DIM, 1)
    v_b = pltpu.roll(v_a, HEAD_DIM, 1)
    ext_lo = lax.broadcasted_iota(jnp.int32, k_a.shape, 1) < HEAD_DIM
    kk = (jnp.where(ext_lo, k_a, k_b).astype(BF16), jnp.where(ext_lo, k_b, k_a).astype(BF16))
    vv = (jnp.where(ext_lo, v_a, v_b).astype(BF16), jnp.where(ext_lo, v_b, v_a).astype(BF16))
    col = lax.broadcasted_iota(jnp.int32, (BLOCK, 2 * BLOCK), 1)
    zero_q = jnp.zeros((BLOCK, LANES), BF16)

    for nb in range(nblk):
        r0 = nb * BLOCK
        for hk in range(SWA_KV_HEADS):
            kband = kk[hk][r0:r0 + 2 * BLOCK]
            vband = vv[hk][r0:r0 + 2 * BLOCK]
            qparts = []
            for pair in range(2):
                c0 = hk * SWA_GROUP * HEAD_DIM + pair * LANES
                qp = qn[r0:r0 + BLOCK, c0:c0 + LANES]
                qparts.append(jnp.where(lane_lo, qp, zero_q))
                qparts.append(jnp.where(lane_lo, zero_q, qp))
            s_all = _dot_nt(jnp.concatenate(qparts, axis=0), kband)
            ps, inv = [], []
            for g in range(SWA_GROUP):
                head = hk * SWA_GROUP + g
                s = s_all[g * BLOCK:(g + 1) * BLOCK] + bias_scr[head]
                if nb == 0:
                    s = jnp.where((col >= BLOCK) | jnp.logical_not(first_of_seq), s, -jnp.inf)
                sink = sinks_ref[head] * LOG2E
                m = jnp.maximum(jnp.max(s, axis=-1, keepdims=True), sink)
                p = jnp.exp2(s - m)
                den = jnp.sum(p, axis=-1, keepdims=True) + jnp.exp2(sink - m)
                ps.append(p.astype(BF16))
                inv.append(1.0 / den)
            o_all = _dot(jnp.concatenate(ps, axis=0), vband)
            for pair in range(2):
                g0 = 2 * pair
                o_pair = jnp.where(lane_lo, o_all[g0 * BLOCK:(g0 + 1) * BLOCK] * inv[g0],
                                   o_all[(g0 + 1) * BLOCK:(g0 + 2) * BLOCK] * inv[g0 + 1])
                c0 = hk * SWA_GROUP * HEAD_DIM + pair * LANES
                y_scr[r0:r0 + BLOCK, c0:c0 + LANES] = o_pair.astype(BF16)

    kext_scr[0:BLOCK, :] = kext_scr[tq:tq + BLOCK, :]
    vext_scr[0:BLOCK, :] = vext_scr[tq:tq + BLOCK, :]

    lane_lo_t = lax.broadcasted_iota(jnp.int32, (tq, LANES), 1) < HEAD_DIM
    zero_t = jnp.zeros((tq, LANES), BF16)
    for pair in range(MEM_HEADS // 2):
        issue_pieces(2)
        qp = mqn[:, pair * LANES:(pair + 1) * LANES]
        qst = jnp.concatenate([jnp.where(lane_lo_t, qp, zero_t), jnp.where(lane_lo_t, zero_t, qp)], axis=0)
        s = _dot_nt(qst, mk_scr[:, pair * LANES:(pair + 1) * LANES])
        m = jnp.max(s, axis=-1, keepdims=True)
        p = jnp.exp2(s - m)
        rinv = 1.0 / jnp.sum(p, axis=-1, keepdims=True)
        o2 = _dot(p.astype(BF16), mv_scr[:, pair * LANES:(pair + 1) * LANES]) * rinv
        o_pair = jnp.where(lane_lo_t, o2[:tq], o2[tq:])
        c0 = SWA_Q_W + GLA_V_W + pair * LANES
        y_scr[:, c0:c0 + LANES] = o_pair.astype(BF16)

    sub_row = lax.broadcasted_iota(jnp.int32, (SUBLANES, GLA_QK_W), 0)
    nsub = GLA_CHUNK // SUBLANES
    eye_qk = (lax.broadcasted_iota(jnp.int32, (GLA_QK_W, GLA_QK_W), 0)
              == lax.broadcasted_iota(jnp.int32, (GLA_QK_W, GLA_QK_W), 1))

    for c in range(nchunk):
        if c % 2 == 0:
            issue_pieces(1)
        r = c * GLA_CHUNK
        bc = b_scr[r:r + GLA_CHUNK, :]
        qc = gq_scr[r:r + GLA_CHUNK, :]
        kc = gk_scr[r:r + GLA_CHUNK, :]
        vc = gv_scr[r:r + GLA_CHUNK, :]
        qs = [qc[m * SUBLANES:(m + 1) * SUBLANES] for m in range(nsub)]
        bs = [bc[m * SUBLANES:(m + 1) * SUBLANES] for m in range(nsub)]
        terms = []
        for j in range(GLA_CHUNK):
            kj = kc[j:j + 1, :]
            bj = bc[j:j + 1, :]
            for m in range(j // SUBLANES, nsub):
                diff = bs[m] - bj
                if m == j // SUBLANES and j % SUBLANES:
                    diff = jnp.where(sub_row >= j % SUBLANES, diff, -jnp.inf)
                terms.append(qs[m] * kj * jnp.exp2(diff))
        a_b = _dot(jnp.concatenate(terms, axis=0).astype(BF16), ind_scr[...])
        acc = [None] * nsub
        t = 0
        for j in range(GLA_CHUNK):
            vj = vc[j:j + 1, :]
            for m in range(j // SUBLANES, nsub):
                contrib = a_b[t * SUBLANES:(t + 1) * SUBLANES] * vj
                acc[m] = contrib if acc[m] is None else acc[m] + contrib
                t += 1
        o_intra = jnp.concatenate(acc, axis=0)
        b_last = bc[GLA_CHUNK - 1:GLA_CHUNK, :]
        st = st_scr[...]
        o_inter = _dot((qc * jnp.exp2(bc)).astype(BF16), st.astype(BF16))
        kd = (kc * jnp.exp2(b_last - bc)).astype(BF16)
        kv = _dot_tn(kd, vc.astype(BF16))
        b_col = jnp.sum(jnp.where(eye_qk, b_last, 0.0), axis=1, keepdims=True)
        st_scr[...] = jnp.exp2(b_col) * st + kv * smask_scr[...]
        go_scr[r:r + GLA_CHUNK, :] = o_intra + o_inter

    issue_pieces(len(pieces))
    on = _headnorm(go_scr[...], e_scr) * og_ref[...]
    y_scr[:, SWA_Q_W:SWA_Q_W + GLA_V_W] = (on * _silu(gate_scr[...])).astype(BF16)

    yprev_scr[...] = y_scr[...]


def _mix(x2d, mem, mix_g, mem_g, w_in_r, w_mem, w_out, qg, kg, mqg, mkg, og, wgu, bg,
         sinks, relb, bkt, *, seq):
    tq = MIX_TQ
    ntiles = x2d.shape[0] // tq
    tiles_per_seq = seq // tq
    last = ntiles - 1
    smem = pl.BlockSpec(memory_space=pltpu.SMEM)
    in_specs = [
        smem, smem,
        pl.BlockSpec((tq, D_MODEL), lambda s: (jnp.maximum(s - 1, 0), 0)),
        pl.BlockSpec((tq, D_MODEL), lambda s: (jnp.minimum(s, last), 0)),
        pl.BlockSpec((tq, D_MODEL), lambda s: (jnp.minimum(s + 1, last), 0)),
        pl.BlockSpec((None, MEM_LEN, D_MODEL), lambda s: (jnp.minimum(s, last) // tiles_per_seq, 0, 0)),
        _const_spec((1, D_MODEL)), _const_spec((1, D_MODEL)),
        _const_spec((D_MODEL, IN_W_PAD)),
        _const_spec((D_MODEL, 2 * MEM_Q_W)),
        _const_spec((MIX_W, D_MODEL)),
        _const_spec((1, SWA_Q_W)), _const_spec((1, SWA_KV_W)),
        _const_spec((1, MEM_Q_W)), _const_spec((1, MEM_Q_W)), _const_spec((1, GLA_V_W)),
        _const_spec((LANES, GLA_QK_W)), _const_spec((1, GLA_QK_W)),
        _const_spec((BLOCK, 2 * BLOCK)),
    ]
    scratch = [
        pltpu.VMEM((SWA_HEADS, BLOCK, 2 * BLOCK), F32),
        pltpu.VMEM((MXU_W, MXU_W), BF16),
        pltpu.VMEM((MXU_W, MXU_W), BF16),
        pltpu.VMEM((GLA_QK_W, GLA_V_W), BF16),
        pltpu.VMEM((GLA_QK_W, GLA_V_W), F32),
        pltpu.VMEM((tq, IN_W_PAD), F32),
        pltpu.VMEM((BLOCK + tq, SWA_KV_W), F32),
        pltpu.VMEM((BLOCK + tq, SWA_KV_W), F32),
        pltpu.VMEM((MEM_LEN, MEM_Q_W), BF16),
        pltpu.VMEM((MEM_LEN, MEM_Q_W), BF16),
        pltpu.VMEM((GLA_QK_W, GLA_V_W), F32),
        pltpu.VMEM((tq, MIX_W), BF16),
        pltpu.VMEM((tq, MIX_W), BF16),
        pltpu.VMEM((tq, D_MODEL), BF16),
        pltpu.VMEM((tq, GLA_QK_W), F32),
        pltpu.VMEM((tq, GLA_QK_W), F32),
        pltpu.VMEM((tq, GLA_V_W), F32),
        pltpu.VMEM((tq, GLA_V_W), F32),
        pltpu.VMEM((tq, GLA_QK_W), F32),
        pltpu.VMEM((tq, GLA_V_W), F32),
    ]
    return pl.pallas_call(
        functools.partial(_mix_kernel, tiles_per_seq=tiles_per_seq, n_tiles=ntiles),
        grid=(ntiles + 1,),
        in_specs=in_specs,
        out_specs=pl.BlockSpec((tq, D_MODEL), lambda s: (jnp.maximum(s - 1, 0), 0)),
        out_shape=jax.ShapeDtypeStruct(x2d.shape, F32),
        scratch_shapes=scratch,
        compiler_params=pltpu.CompilerParams(
            dimension_semantics=("arbitrary",), vmem_limit_bytes=VMEM_LIMIT),
        name="mixing",
    )(sinks, relb, x2d, x2d, x2d, mem, mix_g, mem_g, w_in_r, w_mem, w_out, qg, kg, mqg, mkg, og,
      wgu, bg, bkt)


def _tile_gain(g, reps):
    return jnp.tile(g.astype(F32), reps).reshape(1, -1)


def kernel(x, mem, ffn1_norm, ffn1_w_gate, ffn1_w_up, ffn1_w_down, mix_norm, mem_norm, w_in, w_mem_kv, swa_q_norm, swa_k_norm, swa_sinks, rel_bias, gla_w_gate_up, gla_b_gate, gla_out_norm, mem_q_norm, mem_k_norm, w_out, ffn2_norm, ffn2_w_gate, ffn2_w_up, ffn2_w_down):
    bsz, seq, d = x.shape
    depth = ffn1_norm.shape[0]
    bkt = jnp.asarray(_t5_bucket_table())
    relb = rel_bias.astype(F32).reshape(-1)
    x2d = x.reshape(bsz * seq, d)
    for l in range(depth):
        x2d, (wg2, wu2, wd2, w_out_b, w_mem_b, w_in_r) = _ffn(
            x2d, ffn1_norm[l].reshape(1, d),
            ffn1_w_gate[l].astype(BF16), ffn1_w_up[l].astype(BF16), ffn1_w_down[l].astype(BF16),
            cast=(ffn2_w_gate[l], ffn2_w_up[l], ffn2_w_down[l], w_out[l], w_mem_kv[l]), w_in=w_in[l])
        wgu = jnp.concatenate(
            [gla_w_gate_up[l], jnp.zeros((LANES - GLA_RANK, GLA_QK_W), gla_w_gate_up.dtype)],
            axis=0).astype(BF16)
        x2d = _mix(x2d, mem, mix_norm[l].reshape(1, d), mem_norm[l].reshape(1, d),
                   w_in_r, w_mem_b, w_out_b,
                   _tile_gain(swa_q_norm[l], SWA_HEADS), _tile_gain(swa_k_norm[l], SWA_KV_HEADS),
                   _tile_gain(mem_q_norm[l], MEM_HEADS), _tile_gain(mem_k_norm[l], MEM_HEADS),
                   _tile_gain(gla_out_norm[l], GLA_HEADS), wgu, gla_b_gate[l].reshape(1, -1),
                   swa_sinks[l].astype(F32), relb, bkt, seq=seq)
        x2d, _ = _ffn(x2d, ffn2_norm[l].reshape(1, d), wg2, wu2, wd2)
    return x2d.reshape(bsz, seq, d)
```

```python
import functools
import math

import numpy as np
import jax
import jax.numpy as jnp
from jax import lax
from jax.experimental import pallas as pl
from jax.experimental.pallas import tpu as pltpu

F32 = jnp.float32
BF16 = jnp.bfloat16

D_MODEL = 1024
MEM_LEN = 256
HEAD_DIM = 64
SWA_HEADS = 8
SWA_KV_HEADS = 2
SWA_GROUP = SWA_HEADS // SWA_KV_HEADS
WINDOW = 128
BLOCK = 128
N_BUCKETS = 32
MAX_DISTANCE = 128
GLA_HEADS = 4
GLA_DK = 32
GLA_DV = 64
GLA_RANK = 16
GLA_TAU = 16.0
GLA_CHUNK = 32
MEM_HEADS = 4
D_FF = 2816
EPS = 1e-6
LOG2E = math.log2(math.e)

SWA_Q_W = SWA_HEADS * HEAD_DIM
SWA_KV_W = SWA_KV_HEADS * HEAD_DIM
GLA_QK_W = GLA_HEADS * GLA_DK
GLA_V_W = GLA_HEADS * GLA_DV
MEM_Q_W = MEM_HEADS * HEAD_DIM
MIX_W = SWA_Q_W + GLA_V_W + MEM_Q_W

LANES = 128
SUBLANES = 8
BF16_ROWS = 16

C_SQ = 0
C_SK = C_SQ + SWA_Q_W
C_SV = C_SK + SWA_KV_W
C_GQ = C_SV + SWA_KV_W
C_GK = C_GQ + GLA_QK_W
C_GV = C_GK + GLA_QK_W
C_GG = C_GV + GLA_V_W
C_MQ = C_GG + GLA_V_W
C_LR = C_MQ + MEM_Q_W
IN_W_PAD = C_LR + LANES

FFN_TM = 1024
FFN_TM_FIRST = 512
FFN_CHUNK = 256
MIX_TQ = 512
MXU_W = 256
PIECE_W = MXU_W
VMEM_LIMIT = 56 * 1024 * 1024


def _t5_bucket_table():
    qi = np.arange(BLOCK)[:, None]
    kj = np.arange(2 * BLOCK)[None, :]
    dist = np.maximum(qi + BLOCK - kj, 0)
    max_exact = N_BUCKETS // 2
    d = np.maximum(dist, 1).astype(np.float64)
    val = np.log(d / max_exact) / math.log(MAX_DISTANCE / max_exact) * (N_BUCKETS - max_exact)
    in_win = (qi + BLOCK - kj >= 0) & (qi + BLOCK - kj < WINDOW)
    frac = np.abs(val - np.round(val))
    assert np.all((frac > 1e-6) | (d <= max_exact) | ~in_win)
    large = np.minimum(max_exact + np.floor(val).astype(np.int64), N_BUCKETS - 1)
    bkt = np.where(dist < max_exact, dist, large)
    bkt = np.where(in_win, bkt, -1)
    return bkt.astype(np.int32)


def _rmsnorm(x, g):
    ms = jnp.mean(x * x, axis=-1, keepdims=True)
    return x * lax.rsqrt(ms + EPS) * g


def _dot(a, b):
    return jnp.dot(a, b, preferred_element_type=F32)


def _dot_nt(a, b):
    return lax.dot_general(a, b, (((1,), (1,)), ((), ())), preferred_element_type=F32)


def _dot_tn(a, b):
    return lax.dot_general(a, b, (((0,), (0,)), ((), ())), preferred_element_type=F32)


def _headnorm(x, e_ref):
    x2 = (x * x).astype(BF16)
    w = x.shape[1]
    if w <= MXU_W:
        ms = _dot(x2, e_ref[0:w, 0:w])
    else:
        ms = jnp.concatenate([_dot(x2[:, c:c + MXU_W], e_ref[...]) for c in range(0, w, MXU_W)], axis=1)
    return x * lax.rsqrt(ms + EPS)


def _shift_div(x, d):
    assert d & (d - 1) == 0
    return x >> (d.bit_length() - 1)


def _silu(x):
    return x * (1.0 / (1.0 + jnp.exp(-x)))


def _ffn_first_step(x_ref, g_ref, wg_hbm, wu_hbm, wd_hbm, o_ref,
                    wg_bf, wu_bf, wd_bf, stage_gu, stage_d, sem):
    nch = D_FF // FFN_CHUNK

    def copies(c, slot):
        lo, hi = c * FFN_CHUNK, (c + 1) * FFN_CHUNK
        return (pltpu.make_async_copy(wg_hbm.at[:, lo:hi], stage_gu.at[slot, 0], sem.at[slot, 0]),
                pltpu.make_async_copy(wu_hbm.at[:, lo:hi], stage_gu.at[slot, 1], sem.at[slot, 1]),
                pltpu.make_async_copy(wd_hbm.at[lo:hi, :], stage_d.at[slot], sem.at[slot, 2]))

    for cp in copies(0, 0):
        cp.start()
    x = x_ref[...]
    h = _rmsnorm(x, g_ref[...]).astype(BF16)
    acc = None
    for c in range(nch):
        slot = c % 2
        lo, hi = c * FFN_CHUNK, (c + 1) * FFN_CHUNK
        if c + 1 < nch:
            for cp in copies(c + 1, 1 - slot):
                cp.start()
        for cp in copies(c, slot):
            cp.wait()
        wg_bf[:, lo:hi] = stage_gu[slot, 0].astype(BF16)
        wu_bf[:, lo:hi] = stage_gu[slot, 1].astype(BF16)
        wd_bf[lo:hi, :] = stage_d[slot].astype(BF16)
        g = _dot(h, wg_bf[:, lo:hi])
        u = _dot(h, wu_bf[:, lo:hi])
        part = _dot((_silu(g) * u).astype(BF16), wd_bf[lo:hi, :])
        acc = part if acc is None else acc + part
    o_ref[...] = x + 0.5 * acc


def _ffn_kernel(x_ref, g_ref, wg_ref, wu_ref, wd_ref, *rest, n_cast, has_w_in, f32_weights):
    n_in = n_cast + int(has_w_in)
    cast_in, o_ref, cast_out = rest[:n_in], rest[n_in], rest[n_in + 1:2 * n_in + 1]
    scratch = rest[2 * n_in + 1:]

    def tile_step(wg, wu, wd):
        x = x_ref[...]
        h = _rmsnorm(x, g_ref[...]).astype(BF16)
        g = _dot(h, wg[...])
        u = _dot(h, wu[...])
        y = _dot((_silu(g) * u).astype(BF16), wd[...])
        o_ref[...] = x + 0.5 * y

    if f32_weights:
        @pl.when(pl.program_id(0) == 0)
        def _first():
            _ffn_first_step(x_ref, g_ref, wg_ref, wu_ref, wd_ref, o_ref, *scratch)

        @pl.when(pl.program_id(0) > 0)
        def _rest():
            tile_step(*scratch[:3])
    else:
        tile_step(wg_ref, wu_ref, wd_ref)
    for src, dst in zip(cast_in[:n_cast], cast_out[:n_cast]):
        dst[...] = src[...].astype(BF16)
    if has_w_in:
        src, dst = cast_in[n_cast], cast_out[n_cast]
        w = src[...]
        dst[:, 0:C_MQ] = w[:, 0:C_MQ].astype(BF16)
        dst[:, C_MQ:C_LR] = w[:, C_MQ + GLA_RANK:C_MQ + GLA_RANK + MEM_Q_W].astype(BF16)
        slab = w[:, C_MQ:C_MQ + LANES]
        lane = lax.broadcasted_iota(jnp.int32, slab.shape, 1)
        dst[:, C_LR:C_LR + LANES] = jnp.where(lane < GLA_RANK, slab, 0.0).astype(BF16)


def _const_spec(shape):
    return pl.BlockSpec(shape, lambda *_: (0,) * len(shape), pipeline_mode=pl.Buffered(1))


def _ffn(x2d, gain, wg, wu, wd, *, tm, cast=(), w_in=None):
    t = x2d.shape[0]
    steps = t // tm
    f32_weights = wg.dtype == F32
    assert wu.dtype == wg.dtype and wd.dtype == wg.dtype
    if f32_weights:
        w_specs = [pl.BlockSpec(memory_space=pl.ANY)] * 3
        scratch = [pltpu.VMEM((D_MODEL, D_FF), BF16), pltpu.VMEM((D_MODEL, D_FF), BF16),
                   pltpu.VMEM((D_FF, D_MODEL), BF16),
                   pltpu.VMEM((2, 2, D_MODEL, FFN_CHUNK), F32),
                   pltpu.VMEM((2, FFN_CHUNK, D_MODEL), F32),
                   pltpu.SemaphoreType.DMA((2, 3))]
    else:
        w_specs = [_const_spec((D_MODEL, D_FF)), _const_spec((D_MODEL, D_FF)), _const_spec((D_FF, D_MODEL))]
        scratch = []
    extra_in, extra_specs, extra_out_specs, extra_out_shapes = [], [], [], []
    for w in tuple(cast) + ((w_in,) if w_in is not None else ()):
        rows, cols = w.shape
        nblk = steps
        while rows % nblk or (rows // nblk) % BF16_ROWS:
            nblk //= 2
        blk = rows // nblk
        out_cols = IN_W_PAD if w is w_in else cols
        extra_in.append(w)
        index_map = functools.partial(lambda i, n: (jnp.minimum(i, n - 1), 0), n=nblk)
        extra_specs.append(pl.BlockSpec((blk, cols), index_map))
        extra_out_specs.append(pl.BlockSpec((blk, out_cols), index_map))
        extra_out_shapes.append(jax.ShapeDtypeStruct((rows, out_cols), BF16))
    outs = pl.pallas_call(
        functools.partial(_ffn_kernel, n_cast=len(cast), has_w_in=w_in is not None,
                          f32_weights=f32_weights),
        grid=(steps,),
        in_specs=[pl.BlockSpec((tm, D_MODEL), lambda i: (i, 0)), _const_spec((1, D_MODEL))]
        + w_specs + extra_specs,
        out_specs=[pl.BlockSpec((tm, D_MODEL), lambda i: (i, 0))] + extra_out_specs,
        out_shape=[jax.ShapeDtypeStruct((t, D_MODEL), F32)] + extra_out_shapes,
        scratch_shapes=scratch,
        compiler_params=pltpu.CompilerParams(
            dimension_semantics=("arbitrary",), vmem_limit_bytes=VMEM_LIMIT),
        name="ffn_halfstep",
    )(x2d, gain, wg, wu, wd, *extra_in)
    return outs[0], outs[1:]


def _mix_kernel(*refs, tiles_per_seq, n_tiles):
    step = pl.program_id(0)

    @pl.when(step < n_tiles)
    def _steady():
        _mix_step(*refs, tiles_per_seq=tiles_per_seq)

    @pl.when(step == n_tiles)
    def _drain():
        r = dict(zip(_mix_step.__code__.co_varnames, refs))
        r["o_ref"][...] = r["xp_ref"][...] + _dot(r["yprev_scr"][...], r["wout_ref"][...])


def _mix_step(sinks_ref, relb_ref,
              xp_ref, xc_ref, xn_ref, mem_ref, mixg_ref, memg_ref, win_ref, wmem_ref, wout_ref,
              qg_ref, kg_ref, mqg_ref, mkg_ref, og_ref, wgu_ref, bg_ref, bkt_ref,
              o_ref,
              bias_scr, e_scr, tri_scr, ind_scr, smask_scr,
              proj_scr, kext_scr, vext_scr, mk_scr, mv_scr, st_scr, y_scr, yprev_scr,
              h_scr, gq_scr, gk_scr, gv_scr, gate_scr, b_scr, go_scr,
              *, tiles_per_seq):
    step = pl.program_id(0)
    first_of_seq = lax.rem(step, tiles_per_seq) == 0
    tq = MIX_TQ
    nblk = tq // BLOCK
    nchunk = tq // GLA_CHUNK

    @pl.when(step == 0)
    def _prologue():
        bkt = bkt_ref[...]
        for h in range(SWA_HEADS):
            acc = jnp.zeros((BLOCK, 2 * BLOCK), F32)
            for k in range(N_BUCKETS):
                acc = jnp.where(bkt == k, relb_ref[k * SWA_HEADS + h] * LOG2E, acc)
            bias_scr[h] = jnp.where(bkt >= 0, acc, -jnp.inf)
        r = lax.broadcasted_iota(jnp.int32, (MXU_W, MXU_W), 0)
        c = lax.broadcasted_iota(jnp.int32, (MXU_W, MXU_W), 1)
        same_head = _shift_div(r, HEAD_DIM) == _shift_div(c, HEAD_DIM)
        e_scr[...] = jnp.where(same_head, 1.0 / HEAD_DIM, 0.0).astype(BF16)
        same_chunk = _shift_div(r, GLA_CHUNK) == _shift_div(c, GLA_CHUNK)
        tri_scr[...] = jnp.where(same_chunk & (c <= r), 1.0, 0.0).astype(BF16)
        r = lax.broadcasted_iota(jnp.int32, (GLA_QK_W, GLA_V_W), 0)
        c = lax.broadcasted_iota(jnp.int32, (GLA_QK_W, GLA_V_W), 1)
        same_head = _shift_div(r, GLA_DK) == _shift_div(c, GLA_DV)
        ind_scr[...] = jnp.where(same_head, 1.0, 0.0).astype(BF16)
        smask_scr[...] = jnp.where(same_head, 1.0, 0.0).astype(F32)
        yprev_scr[...] = jnp.zeros(yprev_scr.shape, BF16)
        h0 = _rmsnorm(xc_ref[...], mixg_ref[...]).astype(BF16)
        proj_scr[...] = _dot(h0, win_ref[...])

    @pl.when(first_of_seq)
    def _init_sequence():
        kext_scr[0:BLOCK, :] = jnp.zeros((BLOCK, SWA_KV_W), F32)
        vext_scr[0:BLOCK, :] = jnp.zeros((BLOCK, SWA_KV_W), F32)
        st_scr[...] = jnp.zeros(st_scr.shape, F32)
        hm = _rmsnorm(mem_ref[...], memg_ref[...]).astype(BF16)
        kv = _dot(hm, wmem_ref[...])
        mk = _headnorm(kv[:, :MEM_Q_W], e_scr) * mkg_ref[...]
        mk_scr[...] = mk.astype(BF16)
        mv_scr[...] = kv[:, MEM_Q_W:].astype(BF16)

    def output_piece(c0):
        o_ref[:, c0:c0 + PIECE_W] = (xp_ref[:, c0:c0 + PIECE_W]
                                     + _dot(yprev_scr[...], wout_ref[:, c0:c0 + PIECE_W]))

    out_pieces = [functools.partial(output_piece, c0) for c0 in range(0, D_MODEL, PIECE_W)]

    out_pieces.pop(0)()
    qn = _headnorm(proj_scr[:, C_SQ:C_SQ + SWA_Q_W], e_scr) * qg_ref[...]
    qn = (qn * (HEAD_DIM ** -0.5 * LOG2E)).astype(BF16)
    out_pieces.pop(0)()
    kn = _headnorm(proj_scr[:, C_SK:C_SK + SWA_KV_W], e_scr) * kg_ref[...]
    kext_scr[BLOCK:BLOCK + tq, :] = kn
    vext_scr[BLOCK:BLOCK + tq, :] = proj_scr[:, C_SV:C_SV + SWA_KV_W]
    mqn = _headnorm(proj_scr[:, C_MQ:C_MQ + MEM_Q_W], e_scr) * mqg_ref[...]
    mqn = (mqn * (HEAD_DIM ** -0.5 * LOG2E)).astype(BF16)
    out_pieces.pop(0)()
    z = _dot(proj_scr[:, C_LR:C_LR + LANES].astype(BF16), wgu_ref[...]) + bg_ref[...]
    log_a = (jnp.minimum(z, 0.0) - jnp.log(1.0 + jnp.exp(-jnp.abs(z)))) * (1.0 / GLA_TAU)
    la_hi = log_a.astype(BF16)
    la_lo = (log_a - la_hi.astype(F32)).astype(BF16)
    la2 = jnp.concatenate([la_hi, la_lo], axis=1)
    for r0 in range(0, tq, MXU_W):
        cs = _dot(tri_scr[...], la2[r0:r0 + MXU_W])
        b_scr[r0:r0 + MXU_W, :] = (cs[:, 0:LANES] + cs[:, LANES:2 * LANES]) * LOG2E
    out_pieces.pop(0)()
    assert not out_pieces
    gq_scr[...] = proj_scr[:, C_GQ:C_GQ + GLA_QK_W] * (GLA_DK ** -0.5)
    gk_scr[...] = proj_scr[:, C_GK:C_GK + GLA_QK_W]
    gv_scr[...] = proj_scr[:, C_GV:C_GV + GLA_V_W]
    gate_scr[...] = proj_scr[:, C_GG:C_GG + GLA_V_W]

    h_scr[...] = _rmsnorm(xn_ref[...], mixg_ref[...]).astype(BF16)

    def project_piece(c0):
        c1 = min(c0 + PIECE_W, IN_W_PAD)
        proj_scr[:, c0:c1] = _dot(h_scr[...], win_ref[:, c0:c1])

    pieces = [functools.partial(project_piece, c0) for c0 in range(0, IN_W_PAD, PIECE_W)]

    def issue_pieces(n):
        for _ in range(n):
            if pieces:
                pieces.pop(0)()

    lane_lo = lax.broadcasted_iota(jnp.int32, (BLOCK, LANES), 1) < HEAD_DIM
    k_a = kext_scr[...]
    v_a = vext_scr[...]
    k_b = pltpu.roll(k_a, HEAD_DIM, 1)
    v_b = pltpu.roll(v_a, HEAD_DIM, 1)
    ext_lo = lax.broadcasted_iota(jnp.int32, k_a.shape, 1) < HEAD_DIM
    kk = (jnp.where(ext_lo, k_a, k_b).astype(BF16), jnp.where(ext_lo, k_b, k_a).astype(BF16))
    vv = (jnp.where(ext_lo, v_a, v_b).astype(BF16), jnp.where(ext_lo, v_b, v_a).astype(BF16))
    col = lax.broadcasted_iota(jnp.int32, (BLOCK, 2 * BLOCK), 1)
    zero_q = jnp.zeros((BLOCK, LANES), BF16)

    for nb in range(nblk):
        r0 = nb * BLOCK
        for hk in range(SWA_KV_HEADS):
            kband = kk[hk][r0:r0 + 2 * BLOCK]
            vband = vv[hk][r0:r0 + 2 * BLOCK]
            qparts = []
            for pair in range(2):
                c0 = hk * SWA_GROUP * HEAD_DIM + pair * LANES
                qp = qn[r0:r0 + BLOCK, c0:c0 + LANES]
                qparts.append(jnp.where(lane_lo, qp, zero_q))
                qparts.append(jnp.where(lane_lo, zero_q, qp))
            s_all = _dot_nt(jnp.concatenate(qparts, axis=0), kband)
            ps, inv = [], []
            for g in range(SWA_GROUP):
                head = hk * SWA_GROUP + g
                s = s_all[g * BLOCK:(g + 1) * BLOCK] + bias_scr[head]
                if nb == 0:
                    s = jnp.where((col >= BLOCK) | jnp.logical_not(first_of_seq), s, -jnp.inf)
                sink = sinks_ref[head] * LOG2E
                m = jnp.maximum(jnp.max(s, axis=-1, keepdims=True), sink)
                p = jnp.exp2(s - m)
                den = jnp.sum(p, axis=-1, keepdims=True) + jnp.exp2(sink - m)
                ps.append(p.astype(BF16))
                inv.append(1.0 / den)
            o_all = _dot(jnp.concatenate(ps, axis=0), vband)
            for pair in range(2):
                g0 = 2 * pair
                o_pair = jnp.where(lane_lo, o_all[g0 * BLOCK:(g0 + 1) * BLOCK] * inv[g0],
                                   o_all[(g0 + 1) * BLOCK:(g0 + 2) * BLOCK] * inv[g0 + 1])
                c0 = hk * SWA_GROUP * HEAD_DIM + pair * LANES
                y_scr[r0:r0 + BLOCK, c0:c0 + LANES] = o_pair.astype(BF16)

    kext_scr[0:BLOCK, :] = kext_scr[tq:tq + BLOCK, :]
    vext_scr[0:BLOCK, :] = vext_scr[tq:tq + BLOCK, :]

    lane_lo_t = lax.broadcasted_iota(jnp.int32, (tq, LANES), 1) < HEAD_DIM
    zero_t = jnp.zeros((tq, LANES), BF16)
    for pair in range(MEM_HEADS // 2):
        issue_pieces(1)
        qp = mqn[:, pair * LANES:(pair + 1) * LANES]
        qst = jnp.concatenate([jnp.where(lane_lo_t, qp, zero_t), jnp.where(lane_lo_t, zero_t, qp)], axis=0)
        s = _dot_nt(qst, mk_scr[:, pair * LANES:(pair + 1) * LANES])
        m = jnp.max(s, axis=-1, keepdims=True)
        p = jnp.exp2(s - m)
        rinv = 1.0 / jnp.sum(p, axis=-1, keepdims=True)
        o2 = _dot(p.astype(BF16), mv_scr[:, pair * LANES:(pair + 1) * LANES]) * rinv
        o_pair = jnp.where(lane_lo_t, o2[:tq], o2[tq:])
        c0 = SWA_Q_W + GLA_V_W + pair * LANES
        y_scr[:, c0:c0 + LANES] = o_pair.astype(BF16)

    sub_row = lax.broadcasted_iota(jnp.int32, (SUBLANES, GLA_QK_W), 0)
    nsub = GLA_CHUNK // SUBLANES
    eye_qk = (lax.broadcasted_iota(jnp.int32, (GLA_QK_W, GLA_QK_W), 0)
              == lax.broadcasted_iota(jnp.int32, (GLA_QK_W, GLA_QK_W), 1))

    for c in range(nchunk):
        if c % 3 == 0:
            issue_pieces(1)
        r = c * GLA_CHUNK
        bc = b_scr[r:r + GLA_CHUNK, :]
        qc = gq_scr[r:r + GLA_CHUNK, :]
        kc = gk_scr[r:r + GLA_CHUNK, :]
        vc = gv_scr[r:r + GLA_CHUNK, :]
        qs = [qc[m * SUBLANES:(m + 1) * SUBLANES] for m in range(nsub)]
        bs = [bc[m * SUBLANES:(m + 1) * SUBLANES] for m in range(nsub)]
        terms = []
        for j in range(GLA_CHUNK):
            kj = kc[j:j + 1, :]
            bj = bc[j:j + 1, :]
            for m in range(j // SUBLANES, nsub):
                diff = bs[m] - bj
                if m == j // SUBLANES and j % SUBLANES:
                    diff = jnp.where(sub_row >= j % SUBLANES, diff, -jnp.inf)
                terms.append(qs[m] * kj * jnp.exp2(diff))
        a_b = _dot(jnp.concatenate(terms, axis=0).astype(BF16), ind_scr[...])
        acc = [None] * nsub
        t = 0
        for j in range(GLA_CHUNK):
            vj = vc[j:j + 1, :]
            for m in range(j // SUBLANES, nsub):
                contrib = a_b[t * SUBLANES:(t + 1) * SUBLANES] * vj
                acc[m] = contrib if acc[m] is None else acc[m] + contrib
                t += 1
        o_intra = jnp.concatenate(acc, axis=0)
        b_last = bc[GLA_CHUNK - 1:GLA_CHUNK, :]
        st = st_scr[...]
        o_inter = _dot((qc * jnp.exp2(bc)).astype(BF16), st.astype(BF16))
        kd = (kc * jnp.exp2(b_last - bc)).astype(BF16)
        kv = _dot_tn(kd, vc.astype(BF16))
        b_col = jnp.sum(jnp.where(eye_qk, b_last, 0.0), axis=1, keepdims=True)
        st_scr[...] = jnp.exp2(b_col) * st + kv * smask_scr[...]
        go_scr[r:r + GLA_CHUNK, :] = o_intra + o_inter

    issue_pieces(len(pieces))
    on = _headnorm(go_scr[...], e_scr) * og_ref[...]
    y_scr[:, SWA_Q_W:SWA_Q_W + GLA_V_W] = (on * _silu(gate_scr[...])).astype(BF16)

    yprev_scr[...] = y_scr[...]


def _mix(x2d, mem, mix_g, mem_g, w_in_r, w_mem, w_out, qg, kg, mqg, mkg, og, wgu, bg,
         sinks, relb, bkt, *, seq):
    tq = MIX_TQ
    ntiles = x2d.shape[0] // tq
    tiles_per_seq = seq // tq
    last = ntiles - 1
    smem = pl.BlockSpec(memory_space=pltpu.SMEM)
    in_specs = [
        smem, smem,
        pl.BlockSpec((tq, D_MODEL), lambda s: (jnp.maximum(s - 1, 0), 0)),
        pl.BlockSpec((tq, D_MODEL), lambda s: (jnp.minimum(s, last), 0)),
        pl.BlockSpec((tq, D_MODEL), lambda s: (jnp.minimum(s + 1, last), 0)),
        pl.BlockSpec((None, MEM_LEN, D_MODEL), lambda s: (jnp.minimum(s, last) // tiles_per_seq, 0, 0)),
        _const_spec((1, D_MODEL)), _const_spec((1, D_MODEL)),
        _const_spec((D_MODEL, IN_W_PAD)),
        _const_spec((D_MODEL, 2 * MEM_Q_W)),
        _const_spec((MIX_W, D_MODEL)),
        _const_spec((1, SWA_Q_W)), _const_spec((1, SWA_KV_W)),
        _const_spec((1, MEM_Q_W)), _const_spec((1, MEM_Q_W)), _const_spec((1, GLA_V_W)),
        _const_spec((LANES, GLA_QK_W)), _const_spec((1, GLA_QK_W)),
        _const_spec((BLOCK, 2 * BLOCK)),
    ]
    scratch = [
        pltpu.VMEM((SWA_HEADS, BLOCK, 2 * BLOCK), F32),
        pltpu.VMEM((MXU_W, MXU_W), BF16),
        pltpu.VMEM((MXU_W, MXU_W), BF16),
        pltpu.VMEM((GLA_QK_W, GLA_V_W), BF16),
        pltpu.VMEM((GLA_QK_W, GLA_V_W), F32),
        pltpu.VMEM((tq, IN_W_PAD), F32),
        pltpu.VMEM((BLOCK + tq, SWA_KV_W), F32),
        pltpu.VMEM((BLOCK + tq, SWA_KV_W), F32),
        pltpu.VMEM((MEM_LEN, MEM_Q_W), BF16),
        pltpu.VMEM((MEM_LEN, MEM_Q_W), BF16),
        pltpu.VMEM((GLA_QK_W, GLA_V_W), F32),
        pltpu.VMEM((tq, MIX_W), BF16),
        pltpu.VMEM((tq, MIX_W), BF16),
        pltpu.VMEM((tq, D_MODEL), BF16),
        pltpu.VMEM((tq, GLA_QK_W), F32),
        pltpu.VMEM((tq, GLA_QK_W), F32),
        pltpu.VMEM((tq, GLA_V_W), F32),
        pltpu.VMEM((tq, GLA_V_W), F32),
        pltpu.VMEM((tq, GLA_QK_W), F32),
        pltpu.VMEM((tq, GLA_V_W), F32),
    ]
    return pl.pallas_call(
        functools.partial(_mix_kernel, tiles_per_seq=tiles_per_seq, n_tiles=ntiles),
        grid=(ntiles + 1,),
        in_specs=in_specs,
        out_specs=pl.BlockSpec((tq, D_MODEL), lambda s: (jnp.maximum(s - 1, 0), 0)),
        out_shape=jax.ShapeDtypeStruct(x2d.shape, F32),
        scratch_shapes=scratch,
        compiler_params=pltpu.CompilerParams(
            dimension_semantics=("arbitrary",), vmem_limit_bytes=VMEM_LIMIT),
        name="mixing",
    )(sinks, relb, x2d, x2d, x2d, mem, mix_g, mem_g, w_in_r, w_mem, w_out, qg, kg, mqg, mkg, og,
      wgu, bg, bkt)


def _tile_gain(g, reps):
    return jnp.tile(g.astype(F32), reps).reshape(1, -1)


def kernel(x, mem, ffn1_norm, ffn1_w_gate, ffn1_w_up, ffn1_w_down, mix_norm, mem_norm, w_in, w_mem_kv, swa_q_norm, swa_k_norm, swa_sinks, rel_bias, gla_w_gate_up, gla_b_gate, gla_out_norm, mem_q_norm, mem_k_norm, w_out, ffn2_norm, ffn2_w_gate, ffn2_w_up, ffn2_w_down):
    bsz, seq, d = x.shape
    depth = ffn1_norm.shape[0]
    bkt = jnp.asarray(_t5_bucket_table())
    relb = rel_bias.astype(F32).reshape(-1)
    x2d = x.reshape(bsz * seq, d)
    for l in range(depth):
        x2d, (wg2, wu2, wd2, w_out_b, w_mem_b, w_in_r) = _ffn(
            x2d, ffn1_norm[l].reshape(1, d),
            ffn1_w_gate[l], ffn1_w_up[l], ffn1_w_down[l], tm=FFN_TM_FIRST,
            cast=(ffn2_w_gate[l], ffn2_w_up[l], ffn2_w_down[l], w_out[l], w_mem_kv[l]), w_in=w_in[l])
        wgu = jnp.concatenate(
            [gla_w_gate_up[l], jnp.zeros((LANES - GLA_RANK, GLA_QK_W), gla_w_gate_up.dtype)],
            axis=0).astype(BF16)
        x2d = _mix(x2d, mem, mix_norm[l].reshape(1, d), mem_norm[l].reshape(1, d),
                   w_in_r, w_mem_b, w_out_b,
                   _tile_gain(swa_q_norm[l], SWA_HEADS), _tile_gain(swa_k_norm[l], SWA_KV_HEADS),
                   _tile_gain(mem_q_norm[l], MEM_HEADS), _tile_gain(mem_k_norm[l], MEM_HEADS),
                   _tile_gain(gla_out_norm[l], GLA_HEADS), wgu, gla_b_gate[l].reshape(1, -1),
                   swa_sinks[l].astype(F32), relb, bkt, seq=seq)
        x2d, _ = _ffn(x2d, ffn2_norm[l].reshape(1, d), wg2, wu2, wd2, tm=FFN_TM)
    return x2d.reshape(bsz, seq, d)
```

```python
import functools
import math

import numpy as np
import jax
import jax.numpy as jnp
from jax import lax
from jax.experimental import pallas as pl
from jax.experimental.pallas import tpu as pltpu

F32 = jnp.float32
BF16 = jnp.bfloat16

D_MODEL = 1024
MEM_LEN = 256
HEAD_DIM = 64
SWA_HEADS = 8
SWA_KV_HEADS = 2
SWA_GROUP = SWA_HEADS // SWA_KV_HEADS
WINDOW = 128
BLOCK = 128
N_BUCKETS = 32
MAX_DISTANCE = 128
GLA_HEADS = 4
GLA_DK = 32
GLA_DV = 64
GLA_RANK = 16
GLA_TAU = 16.0
GLA_CHUNK = 32
MEM_HEADS = 4
D_FF = 2816
EPS = 1e-6
LOG2E = math.log2(math.e)

SWA_Q_W = SWA_HEADS * HEAD_DIM
SWA_KV_W = SWA_KV_HEADS * HEAD_DIM
GLA_QK_W = GLA_HEADS * GLA_DK
GLA_V_W = GLA_HEADS * GLA_DV
MEM_Q_W = MEM_HEADS * HEAD_DIM
MIX_W = SWA_Q_W + GLA_V_W + MEM_Q_W

LANES = 128
SUBLANES = 8
BF16_ROWS = 16

C_SQ = 0
C_SK = C_SQ + SWA_Q_W
C_SV = C_SK + SWA_KV_W
C_GQ = C_SV + SWA_KV_W
C_GK = C_GQ + GLA_QK_W
C_GV = C_GK + GLA_QK_W
C_GG = C_GV + GLA_V_W
C_MQ = C_GG + GLA_V_W
C_LR = C_MQ + MEM_Q_W
IN_W_PAD = C_LR + LANES

FFN_TM = 1024
FFN_TM_FIRST = 512
FFN_CHUNK = 256
MIX_TQ = 512
MXU_W = 256
PIECE_W = MXU_W
VMEM_LIMIT = 56 * 1024 * 1024


def _t5_bucket_table():
    qi = np.arange(BLOCK)[:, None]
    kj = np.arange(2 * BLOCK)[None, :]
    dist = np.maximum(qi + BLOCK - kj, 0)
    max_exact = N_BUCKETS // 2
    d = np.maximum(dist, 1).astype(np.float64)
    val = np.log(d / max_exact) / math.log(MAX_DISTANCE / max_exact) * (N_BUCKETS - max_exact)
    in_win = (qi + BLOCK - kj >= 0) & (qi + BLOCK - kj < WINDOW)
    frac = np.abs(val - np.round(val))
    assert np.all((frac > 1e-6) | (d <= max_exact) | ~in_win)
    large = np.minimum(max_exact + np.floor(val).astype(np.int64), N_BUCKETS - 1)
    bkt = np.where(dist < max_exact, dist, large)
    bkt = np.where(in_win, bkt, -1)
    return bkt.astype(np.int32)


def _rmsnorm(x, g):
    ms = jnp.mean(x * x, axis=-1, keepdims=True)
    return x * lax.rsqrt(ms + EPS) * g


def _dot(a, b):
    return jnp.dot(a, b, preferred_element_type=F32)


def _dot_nt(a, b):
    return lax.dot_general(a, b, (((1,), (1,)), ((), ())), preferred_element_type=F32)


def _dot_tn(a, b):
    return lax.dot_general(a, b, (((0,), (0,)), ((), ())), preferred_element_type=F32)


def _headnorm(x, e_ref):
    x2 = (x * x).astype(BF16)
    w = x.shape[1]
    if w <= MXU_W:
        ms = _dot(x2, e_ref[0:w, 0:w])
    else:
        ms = jnp.concatenate([_dot(x2[:, c:c + MXU_W], e_ref[...]) for c in range(0, w, MXU_W)], axis=1)
    return x * lax.rsqrt(ms + EPS)


def _shift_div(x, d):
    assert d & (d - 1) == 0
    return x >> (d.bit_length() - 1)


def _silu(x):
    return x * (1.0 / (1.0 + jnp.exp(-x)))


def _ffn_first_step(is_first, x_ref, g_ref, wg_hbm, wu_hbm, wd_hbm, o_ref,
                    wg_bf, wu_bf, wd_bf, stage_gu, stage_d, h_scr, sem):
    nch = D_FF // FFN_CHUNK

    def copies(c):
        slot, lo, hi = c % 2, c * FFN_CHUNK, (c + 1) * FFN_CHUNK
        return (pltpu.make_async_copy(wg_hbm.at[:, lo:hi], stage_gu.at[slot, 0], sem.at[slot, 0]),
                pltpu.make_async_copy(wu_hbm.at[:, lo:hi], stage_gu.at[slot, 1], sem.at[slot, 1]),
                pltpu.make_async_copy(wd_hbm.at[lo:hi, :], stage_d.at[slot], sem.at[slot, 2]))

    @pl.when(is_first)
    def _start():
        for c in range(2):
            for cp in copies(c):
                cp.start()
        h_scr[...] = _rmsnorm(x_ref[...], g_ref[...]).astype(BF16)

    for c in range(nch):
        @pl.when(is_first)
        def _chunk(c=c):
            slot, lo, hi = c % 2, c * FFN_CHUNK, (c + 1) * FFN_CHUNK
            for cp in copies(c):
                cp.wait()
            wg_bf[:, lo:hi] = stage_gu[slot, 0].astype(BF16)
            wu_bf[:, lo:hi] = stage_gu[slot, 1].astype(BF16)
            wd_bf[lo:hi, :] = stage_d[slot].astype(BF16)
            if c + 2 < nch:
                for cp in copies(c + 2):
                    cp.start()
            g = _dot(h_scr[...], wg_bf[:, lo:hi])
            u = _dot(h_scr[...], wu_bf[:, lo:hi])
            part = _dot((_silu(g) * u).astype(BF16), wd_bf[lo:hi, :])
            o_ref[...] = (x_ref[...] if c == 0 else o_ref[...]) + 0.5 * part


def _ffn_kernel(x_ref, g_ref, wg_ref, wu_ref, wd_ref, *rest, n_cast, has_w_in, f32_weights):
    n_in = n_cast + int(has_w_in)
    cast_in, o_ref, cast_out = rest[:n_in], rest[n_in], rest[n_in + 1:2 * n_in + 1]
    scratch = rest[2 * n_in + 1:]

    def tile_step(wg, wu, wd):
        x = x_ref[...]
        h = _rmsnorm(x, g_ref[...]).astype(BF16)
        g = _dot(h, wg[...])
        u = _dot(h, wu[...])
        y = _dot((_silu(g) * u).astype(BF16), wd[...])
        o_ref[...] = x + 0.5 * y

    if f32_weights:
        _ffn_first_step(pl.program_id(0) == 0, x_ref, g_ref, wg_ref, wu_ref, wd_ref, o_ref, *scratch)

        @pl.when(pl.program_id(0) > 0)
        def _rest():
            tile_step(*scratch[:3])
    else:
        tile_step(wg_ref, wu_ref, wd_ref)
    for src, dst in zip(cast_in[:n_cast], cast_out[:n_cast]):
        dst[...] = src[...].astype(BF16)
    if has_w_in:
        src, dst = cast_in[n_cast], cast_out[n_cast]
        w = src[...]
        dst[:, 0:C_MQ] = w[:, 0:C_MQ].astype(BF16)
        dst[:, C_MQ:C_LR] = w[:, C_MQ + GLA_RANK:C_MQ + GLA_RANK + MEM_Q_W].astype(BF16)
        slab = w[:, C_MQ:C_MQ + LANES]
        lane = lax.broadcasted_iota(jnp.int32, slab.shape, 1)
        dst[:, C_LR:C_LR + LANES] = jnp.where(lane < GLA_RANK, slab, 0.0).astype(BF16)


def _const_spec(shape):
    return pl.BlockSpec(shape, lambda *_: (0,) * len(shape), pipeline_mode=pl.Buffered(1))


def _ffn(x2d, gain, wg, wu, wd, *, tm, cast=(), w_in=None):
    t = x2d.shape[0]
    steps = t // tm
    f32_weights = wg.dtype == F32
    assert wu.dtype == wg.dtype and wd.dtype == wg.dtype
    if f32_weights:
        w_specs = [pl.BlockSpec(memory_space=pl.ANY)] * 3
        scratch = [pltpu.VMEM((D_MODEL, D_FF), BF16), pltpu.VMEM((D_MODEL, D_FF), BF16),
                   pltpu.VMEM((D_FF, D_MODEL), BF16),
                   pltpu.VMEM((2, 2, D_MODEL, FFN_CHUNK), F32),
                   pltpu.VMEM((2, FFN_CHUNK, D_MODEL), F32),
                   pltpu.VMEM((tm, D_MODEL), BF16),
                   pltpu.SemaphoreType.DMA((2, 3))]
    else:
        w_specs = [_const_spec((D_MODEL, D_FF)), _const_spec((D_MODEL, D_FF)), _const_spec((D_FF, D_MODEL))]
        scratch = []
    extra_in, extra_specs, extra_out_specs, extra_out_shapes = [], [], [], []
    for w in tuple(cast) + ((w_in,) if w_in is not None else ()):
        rows, cols = w.shape
        nblk = steps
        while rows % nblk or (rows // nblk) % BF16_ROWS:
            nblk //= 2
        blk = rows // nblk
        out_cols = IN_W_PAD if w is w_in else cols
        extra_in.append(w)
        index_map = functools.partial(lambda i, n: (jnp.minimum(i, n - 1), 0), n=nblk)
        extra_specs.append(pl.BlockSpec((blk, cols), index_map))
        extra_out_specs.append(pl.BlockSpec((blk, out_cols), index_map))
        extra_out_shapes.append(jax.ShapeDtypeStruct((rows, out_cols), BF16))
    outs = pl.pallas_call(
        functools.partial(_ffn_kernel, n_cast=len(cast), has_w_in=w_in is not None,
                          f32_weights=f32_weights),
        grid=(steps,),
        in_specs=[pl.BlockSpec((tm, D_MODEL), lambda i: (i, 0)), _const_spec((1, D_MODEL))]
        + w_specs + extra_specs,
        out_specs=[pl.BlockSpec((tm, D_MODEL), lambda i: (i, 0))] + extra_out_specs,
        out_shape=[jax.ShapeDtypeStruct((t, D_MODEL), F32)] + extra_out_shapes,
        scratch_shapes=scratch,
        compiler_params=pltpu.CompilerParams(
            dimension_semantics=("arbitrary",), vmem_limit_bytes=VMEM_LIMIT),
        name="ffn_halfstep",
    )(x2d, gain, wg, wu, wd, *extra_in)
    return outs[0], outs[1:]


def _mix_kernel(*refs, tiles_per_seq, n_tiles):
    step = pl.program_id(0)

    @pl.when(step < n_tiles)
    def _steady():
        _mix_step(*refs, tiles_per_seq=tiles_per_seq)

    @pl.when(step == n_tiles)
    def _drain():
        r = dict(zip(_mix_step.__code__.co_varnames, refs))
        r["o_ref"][...] = r["xp_ref"][...] + _dot(r["yprev_scr"][...], r["wout_ref"][...])


def _mix_step(sinks_ref, relb_ref,
              xp_ref, xc_ref, xn_ref, mem_ref, mixg_ref, memg_ref, win_ref, wmem_ref, wout_ref,
              qg_ref, kg_ref, mqg_ref, mkg_ref, og_ref, wgu_ref, bg_ref, bkt_ref,
              o_ref,
              bias_scr, e_scr, tri_scr, ind_scr, smask_scr,
              proj_scr, kext_scr, vext_scr, mk_scr, mv_scr, st_scr, y_scr, yprev_scr,
              h_scr, gq_scr, gk_scr, gv_scr, gate_scr, b_scr, go_scr,
              *, tiles_per_seq):
    step = pl.program_id(0)
    first_of_seq = lax.rem(step, tiles_per_seq) == 0
    tq = MIX_TQ
    nblk = tq // BLOCK
    nchunk = tq // GLA_CHUNK

    @pl.when(step == 0)
    def _prologue():
        bkt = bkt_ref[...]
        for h in range(SWA_HEADS):
            acc = jnp.zeros((BLOCK, 2 * BLOCK), F32)
            for k in range(N_BUCKETS):
                acc = jnp.where(bkt == k, relb_ref[k * SWA_HEADS + h] * LOG2E, acc)
            bias_scr[h] = jnp.where(bkt >= 0, acc, -jnp.inf)
        r = lax.broadcasted_iota(jnp.int32, (MXU_W, MXU_W), 0)
        c = lax.broadcasted_iota(jnp.int32, (MXU_W, MXU_W), 1)
        same_head = _shift_div(r, HEAD_DIM) == _shift_div(c, HEAD_DIM)
        e_scr[...] = jnp.where(same_head, 1.0 / HEAD_DIM, 0.0).astype(BF16)
        same_chunk = _shift_div(r, GLA_CHUNK) == _shift_div(c, GLA_CHUNK)
        tri_scr[...] = jnp.where(same_chunk & (c <= r), 1.0, 0.0).astype(BF16)
        r = lax.broadcasted_iota(jnp.int32, (GLA_QK_W, GLA_V_W), 0)
        c = lax.broadcasted_iota(jnp.int32, (GLA_QK_W, GLA_V_W), 1)
        same_head = _shift_div(r, GLA_DK) == _shift_div(c, GLA_DV)
        ind_scr[...] = jnp.where(same_head, 1.0, 0.0).astype(BF16)
        smask_scr[...] = jnp.where(same_head, 1.0, 0.0).astype(F32)
        yprev_scr[...] = jnp.zeros(yprev_scr.shape, BF16)
        h0 = _rmsnorm(xc_ref[...], mixg_ref[...]).astype(BF16)
        proj_scr[...] = _dot(h0, win_ref[...])

    @pl.when(first_of_seq)
    def _init_sequence():
        kext_scr[0:BLOCK, :] = jnp.zeros((BLOCK, SWA_KV_W), F32)
        vext_scr[0:BLOCK, :] = jnp.zeros((BLOCK, SWA_KV_W), F32)
        st_scr[...] = jnp.zeros(st_scr.shape, F32)
        hm = _rmsnorm(mem_ref[...], memg_ref[...]).astype(BF16)
        kv = _dot(hm, wmem_ref[...])
        mk = _headnorm(kv[:, :MEM_Q_W], e_scr) * mkg_ref[...]
        mk_scr[...] = mk.astype(BF16)
        mv_scr[...] = kv[:, MEM_Q_W:].astype(BF16)

    def output_piece(c0):
        o_ref[:, c0:c0 + PIECE_W] = (xp_ref[:, c0:c0 + PIECE_W]
                                     + _dot(yprev_scr[...], wout_ref[:, c0:c0 + PIECE_W]))

    out_pieces = [functools.partial(output_piece, c0) for c0 in range(0, D_MODEL, PIECE_W)]

    out_pieces.pop(0)()
    qn = _headnorm(proj_scr[:, C_SQ:C_SQ + SWA_Q_W], e_scr) * qg_ref[...]
    qn = (qn * (HEAD_DIM ** -0.5 * LOG2E)).astype(BF16)
    out_pieces.pop(0)()
    kn = _headnorm(proj_scr[:, C_SK:C_SK + SWA_KV_W], e_scr) * kg_ref[...]
    kext_scr[BLOCK:BLOCK + tq, :] = kn
    vext_scr[BLOCK:BLOCK + tq, :] = proj_scr[:, C_SV:C_SV + SWA_KV_W]
    mqn = _headnorm(proj_scr[:, C_MQ:C_MQ + MEM_Q_W], e_scr) * mqg_ref[...]
    mqn = (mqn * (HEAD_DIM ** -0.5 * LOG2E)).astype(BF16)
    out_pieces.pop(0)()
    z = _dot(proj_scr[:, C_LR:C_LR + LANES].astype(BF16), wgu_ref[...]) + bg_ref[...]
    log_a = (jnp.minimum(z, 0.0) - jnp.log(1.0 + jnp.exp(-jnp.abs(z)))) * (1.0 / GLA_TAU)
    la_hi = log_a.astype(BF16)
    la_lo = (log_a - la_hi.astype(F32)).astype(BF16)
    la2 = jnp.concatenate([la_hi, la_lo], axis=1)
    for r0 in range(0, tq, MXU_W):
        cs = _dot(tri_scr[...], la2[r0:r0 + MXU_W])
        b_scr[r0:r0 + MXU_W, :] = (cs[:, 0:LANES] + cs[:, LANES:2 * LANES]) * LOG2E
    out_pieces.pop(0)()
    assert not out_pieces
    gq_scr[...] = proj_scr[:, C_GQ:C_GQ + GLA_QK_W] * (GLA_DK ** -0.5)
    gk_scr[...] = proj_scr[:, C_GK:C_GK + GLA_QK_W]
    gv_scr[...] = proj_scr[:, C_GV:C_GV + GLA_V_W]
    gate_scr[...] = proj_scr[:, C_GG:C_GG + GLA_V_W]

    h_scr[...] = _rmsnorm(xn_ref[...], mixg_ref[...]).astype(BF16)

    def project_piece(c0):
        c1 = min(c0 + PIECE_W, IN_W_PAD)
        proj_scr[:, c0:c1] = _dot(h_scr[...], win_ref[:, c0:c1])

    pieces = [functools.partial(project_piece, c0) for c0 in range(0, IN_W_PAD, PIECE_W)]

    def issue_pieces(n):
        for _ in range(n):
            if pieces:
                pieces.pop(0)()

    lane_lo = lax.broadcasted_iota(jnp.int32, (BLOCK, LANES), 1) < HEAD_DIM
    k_a = kext_scr[...]
    v_a = vext_scr[...]
    k_b = pltpu.roll(k_a, HEAD_DIM, 1)
    v_b = pltpu.roll(v_a, HEAD_DIM, 1)
    ext_lo = lax.broadcasted_iota(jnp.int32, k_a.shape, 1) < HEAD_DIM
    kk = (jnp.where(ext_lo, k_a, k_b).astype(BF16), jnp.where(ext_lo, k_b, k_a).astype(BF16))
    vv = (jnp.where(ext_lo, v_a, v_b).astype(BF16), jnp.where(ext_lo, v_b, v_a).astype(BF16))
    col = lax.broadcasted_iota(jnp.int32, (BLOCK, 2 * BLOCK), 1)
    zero_q = jnp.zeros((BLOCK, LANES), BF16)

    for nb in range(nblk):
        r0 = nb * BLOCK
        for hk in range(SWA_KV_HEADS):
            kband = kk[hk][r0:r0 + 2 * BLOCK]
            vband = vv[hk][r0:r0 + 2 * BLOCK]
            qparts = []
            for pair in range(2):
                c0 = hk * SWA_GROUP * HEAD_DIM + pair * LANES
                qp = qn[r0:r0 + BLOCK, c0:c0 + LANES]
                qparts.append(jnp.where(lane_lo, qp, zero_q))
                qparts.append(jnp.where(lane_lo, zero_q, qp))
            s_all = _dot_nt(jnp.concatenate(qparts, axis=0), kband)
            ps, inv = [], []
            for g in range(SWA_GROUP):
                head = hk * SWA_GROUP + g
                s = s_all[g * BLOCK:(g + 1) * BLOCK] + bias_scr[head]
                if nb == 0:
                    s = jnp.where((col >= BLOCK) | jnp.logical_not(first_of_seq), s, -jnp.inf)
                sink = sinks_ref[head] * LOG2E
                m = jnp.maximum(jnp.max(s, axis=-1, keepdims=True), sink)
                p = jnp.exp2(s - m)
                den = jnp.sum(p, axis=-1, keepdims=True) + jnp.exp2(sink - m)
                ps.append(p.astype(BF16))
                inv.append(1.0 / den)
            o_all = _dot(jnp.concatenate(ps, axis=0), vband)
            for pair in range(2):
                g0 = 2 * pair
                o_pair = jnp.where(lane_lo, o_all[g0 * BLOCK:(g0 + 1) * BLOCK] * inv[g0],
                                   o_all[(g0 + 1) * BLOCK:(g0 + 2) * BLOCK] * inv[g0 + 1])
                c0 = hk * SWA_GROUP * HEAD_DIM + pair * LANES
                y_scr[r0:r0 + BLOCK, c0:c0 + LANES] = o_pair.astype(BF16)

    kext_scr[0:BLOCK, :] = kext_scr[tq:tq + BLOCK, :]
    vext_scr[0:BLOCK, :] = vext_scr[tq:tq + BLOCK, :]

    lane_lo_t = lax.broadcasted_iota(jnp.int32, (tq, LANES), 1) < HEAD_DIM
    zero_t = jnp.zeros((tq, LANES), BF16)
    for pair in range(MEM_HEADS // 2):
        issue_pieces(1)
        qp = mqn[:, pair * LANES:(pair + 1) * LANES]
        qst = jnp.concatenate([jnp.where(lane_lo_t, qp, zero_t), jnp.where(lane_lo_t, zero_t, qp)], axis=0)
        s = _dot_nt(qst, mk_scr[:, pair * LANES:(pair + 1) * LANES])
        m = jnp.max(s, axis=-1, keepdims=True)
        p = jnp.exp2(s - m)
        rinv = 1.0 / jnp.sum(p, axis=-1, keepdims=True)
        o2 = _dot(p.astype(BF16), mv_scr[:, pair * LANES:(pair + 1) * LANES]) * rinv
        o_pair = jnp.where(lane_lo_t, o2[:tq], o2[tq:])
        c0 = SWA_Q_W + GLA_V_W + pair * LANES
        y_scr[:, c0:c0 + LANES] = o_pair.astype(BF16)

    sub_row = lax.broadcasted_iota(jnp.int32, (SUBLANES, GLA_QK_W), 0)
    nsub = GLA_CHUNK // SUBLANES
    eye_qk = (lax.broadcasted_iota(jnp.int32, (GLA_QK_W, GLA_QK_W), 0)
              == lax.broadcasted_iota(jnp.int32, (GLA_QK_W, GLA_QK_W), 1))

    for c in range(nchunk):
        if c % 3 == 0:
            issue_pieces(1)
        r = c * GLA_CHUNK
        bc = b_scr[r:r + GLA_CHUNK, :]
        qc = gq_scr[r:r + GLA_CHUNK, :]
        kc = gk_scr[r:r + GLA_CHUNK, :]
        vc = gv_scr[r:r + GLA_CHUNK, :]
        qs = [qc[m * SUBLANES:(m + 1) * SUBLANES] for m in range(nsub)]
        bs = [bc[m * SUBLANES:(m + 1) * SUBLANES] for m in range(nsub)]
        terms = []
        for j in range(GLA_CHUNK):
            kj = kc[j:j + 1, :]
            bj = bc[j:j + 1, :]
            for m in range(j // SUBLANES, nsub):
                diff = bs[m] - bj
                if m == j // SUBLANES and j % SUBLANES:
                    diff = jnp.where(sub_row >= j % SUBLANES, diff, -jnp.inf)
                terms.append(qs[m] * kj * jnp.exp2(diff))
        a_b = _dot(jnp.concatenate(terms, axis=0).astype(BF16), ind_scr[...])
        acc = [None] * nsub
        t = 0
        for j in range(GLA_CHUNK):
            vj = vc[j:j + 1, :]
            for m in range(j // SUBLANES, nsub):
                contrib = a_b[t * SUBLANES:(t + 1) * SUBLANES] * vj
                acc[m] = contrib if acc[m] is None else acc[m] + contrib
                t += 1
        o_intra = jnp.concatenate(acc, axis=0)
        b_last = bc[GLA_CHUNK - 1:GLA_CHUNK, :]
        st = st_scr[...]
        o_inter = _dot((qc * jnp.exp2(bc)).astype(BF16), st.astype(BF16))
        kd = (kc * jnp.exp2(b_last - bc)).astype(BF16)
        kv = _dot_tn(kd, vc.astype(BF16))
        b_col = jnp.sum(jnp.where(eye_qk, b_last, 0.0), axis=1, keepdims=True)
        st_scr[...] = jnp.exp2(b_col) * st + kv * smask_scr[...]
        go_scr[r:r + GLA_CHUNK, :] = o_intra + o_inter

    issue_pieces(len(pieces))
    on = _headnorm(go_scr[...], e_scr) * og_ref[...]
    y_scr[:, SWA_Q_W:SWA_Q_W + GLA_V_W] = (on * _silu(gate_scr[...])).astype(BF16)

    yprev_scr[...] = y_scr[...]


def _mix(x2d, mem, mix_g, mem_g, w_in_r, w_mem, w_out, qg, kg, mqg, mkg, og, wgu, bg,
         sinks, relb, bkt, *, seq):
    tq = MIX_TQ
    ntiles = x2d.shape[0] // tq
    tiles_per_seq = seq // tq
    last = ntiles - 1
    smem = pl.BlockSpec(memory_space=pltpu.SMEM)
    in_specs = [
        smem, smem,
        pl.BlockSpec((tq, D_MODEL), lambda s: (jnp.maximum(s - 1, 0), 0)),
        pl.BlockSpec((tq, D_MODEL), lambda s: (jnp.minimum(s, last), 0)),
        pl.BlockSpec((tq, D_MODEL), lambda s: (jnp.minimum(s + 1, last), 0)),
        pl.BlockSpec((None, MEM_LEN, D_MODEL), lambda s: (jnp.minimum(s, last) // tiles_per_seq, 0, 0)),
        _const_spec((1, D_MODEL)), _const_spec((1, D_MODEL)),
        _const_spec((D_MODEL, IN_W_PAD)),
        _const_spec((D_MODEL, 2 * MEM_Q_W)),
        _const_spec((MIX_W, D_MODEL)),
        _const_spec((1, SWA_Q_W)), _const_spec((1, SWA_KV_W)),
        _const_spec((1, MEM_Q_W)), _const_spec((1, MEM_Q_W)), _const_spec((1, GLA_V_W)),
        _const_spec((LANES, GLA_QK_W)), _const_spec((1, GLA_QK_W)),
        _const_spec((BLOCK, 2 * BLOCK)),
    ]
    scratch = [
        pltpu.VMEM((SWA_HEADS, BLOCK, 2 * BLOCK), F32),
        pltpu.VMEM((MXU_W, MXU_W), BF16),
        pltpu.VMEM((MXU_W, MXU_W), BF16),
        pltpu.VMEM((GLA_QK_W, GLA_V_W), BF16),
        pltpu.VMEM((GLA_QK_W, GLA_V_W), F32),
        pltpu.VMEM((tq, IN_W_PAD), F32),
        pltpu.VMEM((BLOCK + tq, SWA_KV_W), F32),
        pltpu.VMEM((BLOCK + tq, SWA_KV_W), F32),
        pltpu.VMEM((MEM_LEN, MEM_Q_W), BF16),
        pltpu.VMEM((MEM_LEN, MEM_Q_W), BF16),
        pltpu.VMEM((GLA_QK_W, GLA_V_W), F32),
        pltpu.VMEM((tq, MIX_W), BF16),
        pltpu.VMEM((tq, MIX_W), BF16),
        pltpu.VMEM((tq, D_MODEL), BF16),
        pltpu.VMEM((tq, GLA_QK_W), F32),
        pltpu.VMEM((tq, GLA_QK_W), F32),
        pltpu.VMEM((tq, GLA_V_W), F32),
        pltpu.VMEM((tq, GLA_V_W), F32),
        pltpu.VMEM((tq, GLA_QK_W), F32),
        pltpu.VMEM((tq, GLA_V_W), F32),
    ]
    return pl.pallas_call(
        functools.partial(_mix_kernel, tiles_per_seq=tiles_per_seq, n_tiles=ntiles),
        grid=(ntiles + 1,),
        in_specs=in_specs,
        out_specs=pl.BlockSpec((tq, D_MODEL), lambda s: (jnp.maximum(s - 1, 0), 0)),
        out_shape=jax.ShapeDtypeStruct(x2d.shape, F32),
        scratch_shapes=scratch,
        compiler_params=pltpu.CompilerParams(
            dimension_semantics=("arbitrary",), vmem_limit_bytes=VMEM_LIMIT),
        name="mixing",
    )(sinks, relb, x2d, x2d, x2d, mem, mix_g, mem_g, w_in_r, w_mem, w_out, qg, kg, mqg, mkg, og,
      wgu, bg, bkt)


def _tile_gain(g, reps):
    return jnp.tile(g.astype(F32), reps).reshape(1, -1)


def kernel(x, mem, ffn1_norm, ffn1_w_gate, ffn1_w_up, ffn1_w_down, mix_norm, mem_norm, w_in, w_mem_kv, swa_q_norm, swa_k_norm, swa_sinks, rel_bias, gla_w_gate_up, gla_b_gate, gla_out_norm, mem_q_norm, mem_k_norm, w_out, ffn2_norm, ffn2_w_gate, ffn2_w_up, ffn2_w_down):
    bsz, seq, d = x.shape
    depth = ffn1_norm.shape[0]
    bkt = jnp.asarray(_t5_bucket_table())
    relb = rel_bias.astype(F32).reshape(-1)
    x2d = x.reshape(bsz * seq, d)
    for l in range(depth):
        x2d, (wg2, wu2, wd2, w_out_b, w_mem_b, w_in_r) = _ffn(
            x2d, ffn1_norm[l].reshape(1, d),
            ffn1_w_gate[l], ffn1_w_up[l], ffn1_w_down[l], tm=FFN_TM_FIRST,
            cast=(ffn2_w_gate[l], ffn2_w_up[l], ffn2_w_down[l], w_out[l], w_mem_kv[l]), w_in=w_in[l])
        wgu = jnp.concatenate(
            [gla_w_gate_up[l], jnp.zeros((LANES - GLA_RANK, GLA_QK_W), gla_w_gate_up.dtype)],
            axis=0).astype(BF16)
        x2d = _mix(x2d, mem, mix_norm[l].reshape(1, d), mem_norm[l].reshape(1, d),
                   w_in_r, w_mem_b, w_out_b,
                   _tile_gain(swa_q_norm[l], SWA_HEADS), _tile_gain(swa_k_norm[l], SWA_KV_HEADS),
                   _tile_gain(mem_q_norm[l], MEM_HEADS), _tile_gain(mem_k_norm[l], MEM_HEADS),
                   _tile_gain(gla_out_norm[l], GLA_HEADS), wgu, gla_b_gate[l].reshape(1, -1),
                   swa_sinks[l].astype(F32), relb, bkt, seq=seq)
        x2d, _ = _ffn(x2d, ffn2_norm[l].reshape(1, d), wg2, wu2, wd2, tm=FFN_TM)
    return x2d.reshape(bsz, seq, d)
```

```python
import functools
import math

import numpy as np
import jax
import jax.numpy as jnp
from jax import lax
from jax.experimental import pallas as pl
from jax.experimental.pallas import tpu as pltpu

F32 = jnp.float32
BF16 = jnp.bfloat16

D_MODEL = 1024
MEM_LEN = 256
HEAD_DIM = 64
SWA_HEADS = 8
SWA_KV_HEADS = 2
SWA_GROUP = SWA_HEADS // SWA_KV_HEADS
WINDOW = 128
BLOCK = 128
N_BUCKETS = 32
MAX_DISTANCE = 128
GLA_HEADS = 4
GLA_DK = 32
GLA_DV = 64
GLA_RANK = 16
GLA_TAU = 16.0
GLA_CHUNK = 32
MEM_HEADS = 4
D_FF = 2816
EPS = 1e-6
LOG2E = math.log2(math.e)

SWA_Q_W = SWA_HEADS * HEAD_DIM
SWA_KV_W = SWA_KV_HEADS * HEAD_DIM
GLA_QK_W = GLA_HEADS * GLA_DK
GLA_V_W = GLA_HEADS * GLA_DV
MEM_Q_W = MEM_HEADS * HEAD_DIM
MIX_W = SWA_Q_W + GLA_V_W + MEM_Q_W

LANES = 128
SUBLANES = 8
BF16_ROWS = 16

C_SQ = 0
C_SK = C_SQ + SWA_Q_W
C_SV = C_SK + SWA_KV_W
C_GQ = C_SV + SWA_KV_W
C_GK = C_GQ + GLA_QK_W
C_GV = C_GK + GLA_QK_W
C_GG = C_GV + GLA_V_W
C_MQ = C_GG + GLA_V_W
C_LR = C_MQ + MEM_Q_W
IN_W_PAD = C_LR + LANES

FFN_TM = 1024
FFN_TM_FIRST = 512
FFN_CHUNK = 256
MIX_TQ = 512
MXU_W = 256
PIECE_W = MXU_W
VMEM_LIMIT = 56 * 1024 * 1024


def _t5_bucket_table():
    qi = np.arange(BLOCK)[:, None]
    kj = np.arange(2 * BLOCK)[None, :]
    dist = np.maximum(qi + BLOCK - kj, 0)
    max_exact = N_BUCKETS // 2
    d = np.maximum(dist, 1).astype(np.float64)
    val = np.log(d / max_exact) / math.log(MAX_DISTANCE / max_exact) * (N_BUCKETS - max_exact)
    in_win = (qi + BLOCK - kj >= 0) & (qi + BLOCK - kj < WINDOW)
    frac = np.abs(val - np.round(val))
    assert np.all((frac > 1e-6) | (d <= max_exact) | ~in_win)
    large = np.minimum(max_exact + np.floor(val).astype(np.int64), N_BUCKETS - 1)
    bkt = np.where(dist < max_exact, dist, large)
    bkt = np.where(in_win, bkt, -1)
    return bkt.astype(np.int32)


def _rmsnorm(x, g):
    ms = jnp.mean(x * x, axis=-1, keepdims=True)
    return x * lax.rsqrt(ms + EPS) * g


def _dot(a, b):
    return jnp.dot(a, b, preferred_element_type=F32)


def _dot_nt(a, b):
    return lax.dot_general(a, b, (((1,), (1,)), ((), ())), preferred_element_type=F32)


def _dot_tn(a, b):
    return lax.dot_general(a, b, (((0,), (0,)), ((), ())), preferred_element_type=F32)


def _headnorm(x, e_ref):
    x2 = (x * x).astype(BF16)
    w = x.shape[1]
    if w <= MXU_W:
        ms = _dot(x2, e_ref[0:w, 0:w])
    else:
        ms = jnp.concatenate([_dot(x2[:, c:c + MXU_W], e_ref[...]) for c in range(0, w, MXU_W)], axis=1)
    return x * lax.rsqrt(ms + EPS)


def _shift_div(x, d):
    assert d & (d - 1) == 0
    return x >> (d.bit_length() - 1)


def _silu(x):
    return x * (1.0 / (1.0 + jnp.exp(-x)))


def _ffn_first_step(is_first, x_ref, g_ref, wg_hbm, wu_hbm, wd_hbm, o_ref,
                    wg_bf, wu_bf, wd_bf, stage_gu, stage_d, h_scr, sem):
    nch = D_FF // FFN_CHUNK

    def copies(c):
        slot, lo, hi = c % 2, c * FFN_CHUNK, (c + 1) * FFN_CHUNK
        return (pltpu.make_async_copy(wg_hbm.at[:, lo:hi], stage_gu.at[slot, 0], sem.at[slot, 0]),
                pltpu.make_async_copy(wu_hbm.at[:, lo:hi], stage_gu.at[slot, 1], sem.at[slot, 1]),
                pltpu.make_async_copy(wd_hbm.at[lo:hi, :], stage_d.at[slot], sem.at[slot, 2]))

    def start(c):
        for k, cp in enumerate(copies(c)):
            cp.start(priority=k % 2)

    @pl.when(is_first)
    def _start():
        for c in range(2):
            start(c)
        h_scr[...] = _rmsnorm(x_ref[...], g_ref[...]).astype(BF16)

    for c in range(nch):
        @pl.when(is_first)
        def _chunk(c=c):
            slot, lo, hi = c % 2, c * FFN_CHUNK, (c + 1) * FFN_CHUNK
            for cp in copies(c):
                cp.wait()
            wg_bf[:, lo:hi] = stage_gu[slot, 0].astype(BF16)
            wu_bf[:, lo:hi] = stage_gu[slot, 1].astype(BF16)
            wd_bf[lo:hi, :] = stage_d[slot].astype(BF16)
            if c + 2 < nch:
                start(c + 2)
            g = _dot(h_scr[...], wg_bf[:, lo:hi])
            u = _dot(h_scr[...], wu_bf[:, lo:hi])
            part = _dot((_silu(g) * u).astype(BF16), wd_bf[lo:hi, :])
            o_ref[...] = (x_ref[...] if c == 0 else o_ref[...]) + 0.5 * part


def _ffn_kernel(x_ref, g_ref, wg_ref, wu_ref, wd_ref, *rest, n_cast, has_w_in, f32_weights):
    n_in = n_cast + int(has_w_in)
    cast_in, o_ref, cast_out = rest[:n_in], rest[n_in], rest[n_in + 1:2 * n_in + 1]
    scratch = rest[2 * n_in + 1:]

    def tile_step(wg, wu, wd):
        x = x_ref[...]
        h = _rmsnorm(x, g_ref[...]).astype(BF16)
        g = _dot(h, wg[...])
        u = _dot(h, wu[...])
        y = _dot((_silu(g) * u).astype(BF16), wd[...])
        o_ref[...] = x + 0.5 * y

    if f32_weights:
        _ffn_first_step(pl.program_id(0) == 0, x_ref, g_ref, wg_ref, wu_ref, wd_ref, o_ref, *scratch)

        @pl.when(pl.program_id(0) > 0)
        def _rest():
            tile_step(*scratch[:3])
    else:
        tile_step(wg_ref, wu_ref, wd_ref)
    for src, dst in zip(cast_in[:n_cast], cast_out[:n_cast]):
        dst[...] = src[...].astype(BF16)
    if has_w_in:
        src, dst = cast_in[n_cast], cast_out[n_cast]
        w = src[...]
        dst[:, 0:C_MQ] = w[:, 0:C_MQ].astype(BF16)
        dst[:, C_MQ:C_LR] = w[:, C_MQ + GLA_RANK:C_MQ + GLA_RANK + MEM_Q_W].astype(BF16)
        slab = w[:, C_MQ:C_MQ + LANES]
        lane = lax.broadcasted_iota(jnp.int32, slab.shape, 1)
        dst[:, C_LR:C_LR + LANES] = jnp.where(lane < GLA_RANK, slab, 0.0).astype(BF16)


def _const_spec(shape):
    return pl.BlockSpec(shape, lambda *_: (0,) * len(shape), pipeline_mode=pl.Buffered(1))


def _ffn(x2d, gain, wg, wu, wd, *, tm, cast=(), w_in=None):
    t = x2d.shape[0]
    steps = t // tm
    f32_weights = wg.dtype == F32
    assert wu.dtype == wg.dtype and wd.dtype == wg.dtype
    if f32_weights:
        w_specs = [pl.BlockSpec(memory_space=pl.ANY)] * 3
        scratch = [pltpu.VMEM((D_MODEL, D_FF), BF16), pltpu.VMEM((D_MODEL, D_FF), BF16),
                   pltpu.VMEM((D_FF, D_MODEL), BF16),
                   pltpu.VMEM((2, 2, D_MODEL, FFN_CHUNK), F32),
                   pltpu.VMEM((2, FFN_CHUNK, D_MODEL), F32),
                   pltpu.VMEM((tm, D_MODEL), BF16),
                   pltpu.SemaphoreType.DMA((2, 3))]
    else:
        w_specs = [_const_spec((D_MODEL, D_FF)), _const_spec((D_MODEL, D_FF)), _const_spec((D_FF, D_MODEL))]
        scratch = []
    extra_in, extra_specs, extra_out_specs, extra_out_shapes = [], [], [], []
    for w in tuple(cast) + ((w_in,) if w_in is not None else ()):
        rows, cols = w.shape
        nblk = steps
        while rows % nblk or (rows // nblk) % BF16_ROWS:
            nblk //= 2
        blk = rows // nblk
        out_cols = IN_W_PAD if w is w_in else cols
        extra_in.append(w)
        index_map = functools.partial(lambda i, n: (jnp.minimum(i, n - 1), 0), n=nblk)
        extra_specs.append(pl.BlockSpec((blk, cols), index_map))
        extra_out_specs.append(pl.BlockSpec((blk, out_cols), index_map))
        extra_out_shapes.append(jax.ShapeDtypeStruct((rows, out_cols), BF16))
    outs = pl.pallas_call(
        functools.partial(_ffn_kernel, n_cast=len(cast), has_w_in=w_in is not None,
                          f32_weights=f32_weights),
        grid=(steps,),
        in_specs=[pl.BlockSpec((tm, D_MODEL), lambda i: (i, 0)), _const_spec((1, D_MODEL))]
        + w_specs + extra_specs,
        out_specs=[pl.BlockSpec((tm, D_MODEL), lambda i: (i, 0))] + extra_out_specs,
        out_shape=[jax.ShapeDtypeStruct((t, D_MODEL), F32)] + extra_out_shapes,
        scratch_shapes=scratch,
        compiler_params=pltpu.CompilerParams(
            dimension_semantics=("arbitrary",), vmem_limit_bytes=VMEM_LIMIT),
        name="ffn_halfstep",
    )(x2d, gain, wg, wu, wd, *extra_in)
    return outs[0], outs[1:]


def _mix_kernel(*refs, tiles_per_seq, n_tiles):
    step = pl.program_id(0)

    @pl.when(step < n_tiles)
    def _steady():
        _mix_step(*refs, tiles_per_seq=tiles_per_seq)

    @pl.when(step == n_tiles)
    def _drain():
        r = dict(zip(_mix_step.__code__.co_varnames, refs))
        r["o_ref"][...] = r["xp_ref"][...] + _dot(r["yprev_scr"][...], r["wout_ref"][...])


def _mix_step(sinks_ref, relb_ref,
              xp_ref, xn_ref, mem_ref, mixg_ref, memg_ref, win_ref, wmem_ref, wout_ref,
              qg_ref, kg_ref, mqg_ref, mkg_ref, og_ref, wgu_ref, bg_ref, bkt_ref,
              o_ref,
              bias_scr, e_scr, tri_scr, ind_scr, smask_scr,
              proj_scr, kext_scr, vext_scr, mk_scr, mv_scr, st_scr, y_scr, yprev_scr,
              h_scr, gq_scr, gk_scr, gv_scr, gate_scr, b_scr, go_scr,
              *, tiles_per_seq):
    step = pl.program_id(0)
    first_of_seq = lax.rem(step, tiles_per_seq) == 0
    tq = MIX_TQ
    nblk = tq // BLOCK
    nchunk = tq // GLA_CHUNK

    @pl.when(step == 0)
    def _prologue():
        bkt = bkt_ref[...]
        for h in range(SWA_HEADS):
            acc = jnp.zeros((BLOCK, 2 * BLOCK), F32)
            for k in range(N_BUCKETS):
                acc = jnp.where(bkt == k, relb_ref[k * SWA_HEADS + h] * LOG2E, acc)
            bias_scr[h] = jnp.where(bkt >= 0, acc, -jnp.inf)
        r = lax.broadcasted_iota(jnp.int32, (MXU_W, MXU_W), 0)
        c = lax.broadcasted_iota(jnp.int32, (MXU_W, MXU_W), 1)
        same_head = _shift_div(r, HEAD_DIM) == _shift_div(c, HEAD_DIM)
        e_scr[...] = jnp.where(same_head, 1.0 / HEAD_DIM, 0.0).astype(BF16)
        same_chunk = _shift_div(r, GLA_CHUNK) == _shift_div(c, GLA_CHUNK)
        tri_scr[...] = jnp.where(same_chunk & (c <= r), 1.0, 0.0).astype(BF16)
        r = lax.broadcasted_iota(jnp.int32, (GLA_QK_W, GLA_V_W), 0)
        c = lax.broadcasted_iota(jnp.int32, (GLA_QK_W, GLA_V_W), 1)
        same_head = _shift_div(r, GLA_DK) == _shift_div(c, GLA_DV)
        ind_scr[...] = jnp.where(same_head, 1.0, 0.0).astype(BF16)
        smask_scr[...] = jnp.where(same_head, 1.0, 0.0).astype(F32)
        yprev_scr[...] = jnp.zeros(yprev_scr.shape, BF16)
        h0 = _rmsnorm(xp_ref[...], mixg_ref[...]).astype(BF16)
        proj_scr[...] = _dot(h0, win_ref[...])

    @pl.when(first_of_seq)
    def _init_sequence():
        kext_scr[0:BLOCK, :] = jnp.zeros((BLOCK, SWA_KV_W), F32)
        vext_scr[0:BLOCK, :] = jnp.zeros((BLOCK, SWA_KV_W), F32)
        st_scr[...] = jnp.zeros(st_scr.shape, F32)
        hm = _rmsnorm(mem_ref[...], memg_ref[...]).astype(BF16)
        kv = _dot(hm, wmem_ref[...])
        mk = _headnorm(kv[:, :MEM_Q_W], e_scr) * mkg_ref[...]
        mk_scr[...] = mk.astype(BF16)
        mv_scr[...] = kv[:, MEM_Q_W:].astype(BF16)

    def output_piece(c0):
        o_ref[:, c0:c0 + PIECE_W] = (xp_ref[:, c0:c0 + PIECE_W]
                                     + _dot(yprev_scr[...], wout_ref[:, c0:c0 + PIECE_W]))

    out_pieces = [functools.partial(output_piece, c0) for c0 in range(0, D_MODEL, PIECE_W)]

    out_pieces.pop(0)()
    qn = _headnorm(proj_scr[:, C_SQ:C_SQ + SWA_Q_W], e_scr) * qg_ref[...]
    qn = (qn * (HEAD_DIM ** -0.5 * LOG2E)).astype(BF16)
    out_pieces.pop(0)()
    kn = _headnorm(proj_scr[:, C_SK:C_SK + SWA_KV_W], e_scr) * kg_ref[...]
    kext_scr[BLOCK:BLOCK + tq, :] = kn
    vext_scr[BLOCK:BLOCK + tq, :] = proj_scr[:, C_SV:C_SV + SWA_KV_W]
    mqn = _headnorm(proj_scr[:, C_MQ:C_MQ + MEM_Q_W], e_scr) * mqg_ref[...]
    mqn = (mqn * (HEAD_DIM ** -0.5 * LOG2E)).astype(BF16)
    out_pieces.pop(0)()
    z = _dot(proj_scr[:, C_LR:C_LR + LANES].astype(BF16), wgu_ref[...]) + bg_ref[...]
    log_a = (jnp.minimum(z, 0.0) - jnp.log(1.0 + jnp.exp(-jnp.abs(z)))) * (1.0 / GLA_TAU)
    la_hi = log_a.astype(BF16)
    la_lo = (log_a - la_hi.astype(F32)).astype(BF16)
    la2 = jnp.concatenate([la_hi, la_lo], axis=1)
    for r0 in range(0, tq, MXU_W):
        cs = _dot(tri_scr[...], la2[r0:r0 + MXU_W])
        b_scr[r0:r0 + MXU_W, :] = (cs[:, 0:LANES] + cs[:, LANES:2 * LANES]) * LOG2E
    out_pieces.pop(0)()
    assert not out_pieces
    gq_scr[...] = proj_scr[:, C_GQ:C_GQ + GLA_QK_W] * (GLA_DK ** -0.5)
    gk_scr[...] = proj_scr[:, C_GK:C_GK + GLA_QK_W]
    gv_scr[...] = proj_scr[:, C_GV:C_GV + GLA_V_W]
    gate_scr[...] = proj_scr[:, C_GG:C_GG + GLA_V_W]

    h_scr[...] = _rmsnorm(xn_ref[...], mixg_ref[...]).astype(BF16)

    def project_piece(c0):
        c1 = min(c0 + PIECE_W, IN_W_PAD)
        proj_scr[:, c0:c1] = _dot(h_scr[...], win_ref[:, c0:c1])

    pieces = [functools.partial(project_piece, c0) for c0 in range(0, IN_W_PAD, PIECE_W)]

    def issue_pieces(n):
        for _ in range(n):
            if pieces:
                pieces.pop(0)()

    lane_lo = lax.broadcasted_iota(jnp.int32, (BLOCK, LANES), 1) < HEAD_DIM
    k_a = kext_scr[...]
    v_a = vext_scr[...]
    k_b = pltpu.roll(k_a, HEAD_DIM, 1)
    v_b = pltpu.roll(v_a, HEAD_DIM, 1)
    ext_lo = lax.broadcasted_iota(jnp.int32, k_a.shape, 1) < HEAD_DIM
    kk = (jnp.where(ext_lo, k_a, k_b).astype(BF16), jnp.where(ext_lo, k_b, k_a).astype(BF16))
    vv = (jnp.where(ext_lo, v_a, v_b).astype(BF16), jnp.where(ext_lo, v_b, v_a).astype(BF16))
    col = lax.broadcasted_iota(jnp.int32, (BLOCK, 2 * BLOCK), 1)
    zero_q = jnp.zeros((BLOCK, LANES), BF16)

    for nb in range(nblk):
        r0 = nb * BLOCK
        for hk in range(SWA_KV_HEADS):
            kband = kk[hk][r0:r0 + 2 * BLOCK]
            vband = vv[hk][r0:r0 + 2 * BLOCK]
            qparts = []
            for pair in range(2):
                c0 = hk * SWA_GROUP * HEAD_DIM + pair * LANES
                qp = qn[r0:r0 + BLOCK, c0:c0 + LANES]
                qparts.append(jnp.where(lane_lo, qp, zero_q))
                qparts.append(jnp.where(lane_lo, zero_q, qp))
            s_all = _dot_nt(jnp.concatenate(qparts, axis=0), kband)
            ps, inv = [], []
            for g in range(SWA_GROUP):
                head = hk * SWA_GROUP + g
                s = s_all[g * BLOCK:(g + 1) * BLOCK] + bias_scr[head]
                if nb == 0:
                    s = jnp.where((col >= BLOCK) | jnp.logical_not(first_of_seq), s, -jnp.inf)
                sink = sinks_ref[head] * LOG2E
                m = jnp.maximum(jnp.max(s, axis=-1, keepdims=True), sink)
                p = jnp.exp2(s - m)
                den = jnp.sum(p, axis=-1, keepdims=True) + jnp.exp2(sink - m)
                ps.append(p.astype(BF16))
                inv.append(1.0 / den)
            o_all = _dot(jnp.concatenate(ps, axis=0), vband)
            for pair in range(2):
                g0 = 2 * pair
                o_pair = jnp.where(lane_lo, o_all[g0 * BLOCK:(g0 + 1) * BLOCK] * inv[g0],
                                   o_all[(g0 + 1) * BLOCK:(g0 + 2) * BLOCK] * inv[g0 + 1])
                c0 = hk * SWA_GROUP * HEAD_DIM + pair * LANES
                y_scr[r0:r0 + BLOCK, c0:c0 + LANES] = o_pair.astype(BF16)

    kext_scr[0:BLOCK, :] = kext_scr[tq:tq + BLOCK, :]
    vext_scr[0:BLOCK, :] = vext_scr[tq:tq + BLOCK, :]

    lane_lo_t = lax.broadcasted_iota(jnp.int32, (tq, LANES), 1) < HEAD_DIM
    zero_t = jnp.zeros((tq, LANES), BF16)
    for pair in range(MEM_HEADS // 2):
        issue_pieces(1)
        qp = mqn[:, pair * LANES:(pair + 1) * LANES]
        qst = jnp.concatenate([jnp.where(lane_lo_t, qp, zero_t), jnp.where(lane_lo_t, zero_t, qp)], axis=0)
        s = _dot_nt(qst, mk_scr[:, pair * LANES:(pair + 1) * LANES])
        m = jnp.max(s, axis=-1, keepdims=True)
        p = jnp.exp2(s - m)
        rinv = 1.0 / jnp.sum(p, axis=-1, keepdims=True)
        o2 = _dot(p.astype(BF16), mv_scr[:, pair * LANES:(pair + 1) * LANES]) * rinv
        o_pair = jnp.where(lane_lo_t, o2[:tq], o2[tq:])
        c0 = SWA_Q_W + GLA_V_W + pair * LANES
        y_scr[:, c0:c0 + LANES] = o_pair.astype(BF16)

    sub_row = lax.broadcasted_iota(jnp.int32, (SUBLANES, GLA_QK_W), 0)
    nsub = GLA_CHUNK // SUBLANES
    eye_qk = (lax.broadcasted_iota(jnp.int32, (GLA_QK_W, GLA_QK_W), 0)
              == lax.broadcasted_iota(jnp.int32, (GLA_QK_W, GLA_QK_W), 1))

    for c in range(nchunk):
        if c % 3 == 0:
            issue_pieces(1)
        r = c * GLA_CHUNK
        bc = b_scr[r:r + GLA_CHUNK, :]
        qc = gq_scr[r:r + GLA_CHUNK, :]
        kc = gk_scr[r:r + GLA_CHUNK, :]
        vc = gv_scr[r:r + GLA_CHUNK, :]
        qs = [qc[m * SUBLANES:(m + 1) * SUBLANES] for m in range(nsub)]
        bs = [bc[m * SUBLANES:(m + 1) * SUBLANES] for m in range(nsub)]
        terms = []
        for j in range(GLA_CHUNK):
            kj = kc[j:j + 1, :]
            bj = bc[j:j + 1, :]
            for m in range(j // SUBLANES, nsub):
                diff = bs[m] - bj
                if m == j // SUBLANES and j % SUBLANES:
                    diff = jnp.where(sub_row >= j % SUBLANES, diff, -jnp.inf)
                terms.append(qs[m] * kj * jnp.exp2(diff))
        a_b = _dot(jnp.concatenate(terms, axis=0).astype(BF16), ind_scr[...])
        acc = [None] * nsub
        t = 0
        for j in range(GLA_CHUNK):
            vj = vc[j:j + 1, :]
            for m in range(j // SUBLANES, nsub):
                contrib = a_b[t * SUBLANES:(t + 1) * SUBLANES] * vj
                acc[m] = contrib if acc[m] is None else acc[m] + contrib
                t += 1
        o_intra = jnp.concatenate(acc, axis=0)
        b_last = bc[GLA_CHUNK - 1:GLA_CHUNK, :]
        st = st_scr[...]
        o_inter = _dot((qc * jnp.exp2(bc)).astype(BF16), st.astype(BF16))
        kd = (kc * jnp.exp2(b_last - bc)).astype(BF16)
        kv = _dot_tn(kd, vc.astype(BF16))
        b_col = jnp.sum(jnp.where(eye_qk, b_last, 0.0), axis=1, keepdims=True)
        st_scr[...] = jnp.exp2(b_col) * st + kv * smask_scr[...]
        go_scr[r:r + GLA_CHUNK, :] = o_intra + o_inter

    issue_pieces(len(pieces))
    on = _headnorm(go_scr[...], e_scr) * og_ref[...]
    y_scr[:, SWA_Q_W:SWA_Q_W + GLA_V_W] = (on * _silu(gate_scr[...])).astype(BF16)

    yprev_scr[...] = y_scr[...]


def _mix(x2d, mem, mix_g, mem_g, w_in_r, w_mem, w_out, qg, kg, mqg, mkg, og, wgu, bg,
         sinks, relb, bkt, *, seq):
    tq = MIX_TQ
    ntiles = x2d.shape[0] // tq
    tiles_per_seq = seq // tq
    last = ntiles - 1
    smem = pl.BlockSpec(memory_space=pltpu.SMEM)
    in_specs = [
        smem, smem,
        pl.BlockSpec((tq, D_MODEL), lambda s: (jnp.maximum(s - 1, 0), 0)),
        pl.BlockSpec((tq, D_MODEL), lambda s: (jnp.minimum(s + 1, last), 0)),
        pl.BlockSpec((None, MEM_LEN, D_MODEL), lambda s: (jnp.minimum(s, last) // tiles_per_seq, 0, 0)),
        _const_spec((1, D_MODEL)), _const_spec((1, D_MODEL)),
        _const_spec((D_MODEL, IN_W_PAD)),
        _const_spec((D_MODEL, 2 * MEM_Q_W)),
        _const_spec((MIX_W, D_MODEL)),
        _const_spec((1, SWA_Q_W)), _const_spec((1, SWA_KV_W)),
        _const_spec((1, MEM_Q_W)), _const_spec((1, MEM_Q_W)), _const_spec((1, GLA_V_W)),
        _const_spec((LANES, GLA_QK_W)), _const_spec((1, GLA_QK_W)),
        _const_spec((BLOCK, 2 * BLOCK)),
    ]
    scratch = [
        pltpu.VMEM((SWA_HEADS, BLOCK, 2 * BLOCK), F32),
        pltpu.VMEM((MXU_W, MXU_W), BF16),
        pltpu.VMEM((MXU_W, MXU_W), BF16),
        pltpu.VMEM((GLA_QK_W, GLA_V_W), BF16),
        pltpu.VMEM((GLA_QK_W, GLA_V_W), F32),
        pltpu.VMEM((tq, IN_W_PAD), F32),
        pltpu.VMEM((BLOCK + tq, SWA_KV_W), F32),
        pltpu.VMEM((BLOCK + tq, SWA_KV_W), F32),
        pltpu.VMEM((MEM_LEN, MEM_Q_W), BF16),
        pltpu.VMEM((MEM_LEN, MEM_Q_W), BF16),
        pltpu.VMEM((GLA_QK_W, GLA_V_W), F32),
        pltpu.VMEM((tq, MIX_W), BF16),
        pltpu.VMEM((tq, MIX_W), BF16),
        pltpu.VMEM((tq, D_MODEL), BF16),
        pltpu.VMEM((tq, GLA_QK_W), F32),
        pltpu.VMEM((tq, GLA_QK_W), F32),
        pltpu.VMEM((tq, GLA_V_W), F32),
        pltpu.VMEM((tq, GLA_V_W), F32),
        pltpu.VMEM((tq, GLA_QK_W), F32),
        pltpu.VMEM((tq, GLA_V_W), F32),
    ]
    return pl.pallas_call(
        functools.partial(_mix_kernel, tiles_per_seq=tiles_per_seq, n_tiles=ntiles),
        grid=(ntiles + 1,),
        in_specs=in_specs,
        out_specs=pl.BlockSpec((tq, D_MODEL), lambda s: (jnp.maximum(s - 1, 0), 0)),
        out_shape=jax.ShapeDtypeStruct(x2d.shape, F32),
        scratch_shapes=scratch,
        compiler_params=pltpu.CompilerParams(
            dimension_semantics=("arbitrary",), vmem_limit_bytes=VMEM_LIMIT),
        name="mixing",
    )(sinks, relb, x2d, x2d, mem, mix_g, mem_g, w_in_r, w_mem, w_out, qg, kg, mqg, mkg, og,
      wgu, bg, bkt)


def _tile_gain(g, reps):
    return jnp.tile(g.astype(F32), reps).reshape(1, -1)


def kernel(x, mem, ffn1_norm, ffn1_w_gate, ffn1_w_up, ffn1_w_down, mix_norm, mem_norm, w_in, w_mem_kv, swa_q_norm, swa_k_norm, swa_sinks, rel_bias, gla_w_gate_up, gla_b_gate, gla_out_norm, mem_q_norm, mem_k_norm, w_out, ffn2_norm, ffn2_w_gate, ffn2_w_up, ffn2_w_down):
    bsz, seq, d = x.shape
    depth = ffn1_norm.shape[0]
    bkt = jnp.asarray(_t5_bucket_table())
    relb = rel_bias.astype(F32).reshape(-1)
    x2d = x.reshape(bsz * seq, d)
    for l in range(depth):
        x2d, (wg2, wu2, wd2, w_out_b, w_mem_b, w_in_r) = _ffn(
            x2d, ffn1_norm[l].reshape(1, d),
            ffn1_w_gate[l], ffn1_w_up[l], ffn1_w_down[l], tm=FFN_TM_FIRST,
            cast=(ffn2_w_gate[l], ffn2_w_up[l], ffn2_w_down[l], w_out[l], w_mem_kv[l]), w_in=w_in[l])
        wgu = jnp.concatenate(
            [gla_w_gate_up[l], jnp.zeros((LANES - GLA_RANK, GLA_QK_W), gla_w_gate_up.dtype)],
            axis=0).astype(BF16)
        x2d = _mix(x2d, mem, mix_norm[l].reshape(1, d), mem_norm[l].reshape(1, d),
                   w_in_r, w_mem_b, w_out_b,
                   _tile_gain(swa_q_norm[l], SWA_HEADS), _tile_gain(swa_k_norm[l], SWA_KV_HEADS),
                   _tile_gain(mem_q_norm[l], MEM_HEADS), _tile_gain(mem_k_norm[l], MEM_HEADS),
                   _tile_gain(gla_out_norm[l], GLA_HEADS), wgu, gla_b_gate[l].reshape(1, -1),
                   swa_sinks[l].astype(F32), relb, bkt, seq=seq)
        x2d, _ = _ffn(x2d, ffn2_norm[l].reshape(1, d), wg2, wu2, wd2, tm=FFN_TM)
    return x2d.reshape(bsz, seq, d)
```

```python
import functools
import math

import numpy as np
import jax
import jax.numpy as jnp
from jax import lax
from jax.experimental import pallas as pl
from jax.experimental.pallas import tpu as pltpu

F32 = jnp.float32
BF16 = jnp.bfloat16

D_MODEL = 1024
MEM_LEN = 256
HEAD_DIM = 64
SWA_HEADS = 8
SWA_KV_HEADS = 2
SWA_GROUP = SWA_HEADS // SWA_KV_HEADS
WINDOW = 128
BLOCK = 128
N_BUCKETS = 32
MAX_DISTANCE = 128
GLA_HEADS = 4
GLA_DK = 32
GLA_DV = 64
GLA_RANK = 16
GLA_TAU = 16.0
GLA_CHUNK = 32
MEM_HEADS = 4
D_FF = 2816
EPS = 1e-6
LOG2E = math.log2(math.e)

SWA_Q_W = SWA_HEADS * HEAD_DIM
SWA_KV_W = SWA_KV_HEADS * HEAD_DIM
GLA_QK_W = GLA_HEADS * GLA_DK
GLA_V_W = GLA_HEADS * GLA_DV
MEM_Q_W = MEM_HEADS * HEAD_DIM
MIX_W = SWA_Q_W + GLA_V_W + MEM_Q_W

LANES = 128
SUBLANES = 8
BF16_ROWS = 16

C_SQ = 0
C_SK = C_SQ + SWA_Q_W
C_SV = C_SK + SWA_KV_W
C_GQ = C_SV + SWA_KV_W
C_GK = C_GQ + GLA_QK_W
C_GV = C_GK + GLA_QK_W
C_GG = C_GV + GLA_V_W
C_MQ = C_GG + GLA_V_W
C_LR = C_MQ + MEM_Q_W
IN_W_PAD = C_LR + LANES

FFN_TM = 1024
FFN_TM_FIRST = 512
FFN_CHUNK = 256
FFN_COPY_SPLIT = 4
MIX_TQ = 512
MXU_W = 256
PIECE_W = MXU_W
VMEM_LIMIT = 56 * 1024 * 1024


def _t5_bucket_table():
    qi = np.arange(BLOCK)[:, None]
    kj = np.arange(2 * BLOCK)[None, :]
    dist = np.maximum(qi + BLOCK - kj, 0)
    max_exact = N_BUCKETS // 2
    d = np.maximum(dist, 1).astype(np.float64)
    val = np.log(d / max_exact) / math.log(MAX_DISTANCE / max_exact) * (N_BUCKETS - max_exact)
    in_win = (qi + BLOCK - kj >= 0) & (qi + BLOCK - kj < WINDOW)
    frac = np.abs(val - np.round(val))
    assert np.all((frac > 1e-6) | (d <= max_exact) | ~in_win)
    large = np.minimum(max_exact + np.floor(val).astype(np.int64), N_BUCKETS - 1)
    bkt = np.where(dist < max_exact, dist, large)
    bkt = np.where(in_win, bkt, -1)
    return bkt.astype(np.int32)


def _rmsnorm(x, g):
    ms = jnp.mean(x * x, axis=-1, keepdims=True)
    return x * lax.rsqrt(ms + EPS) * g


def _dot(a, b):
    return jnp.dot(a, b, preferred_element_type=F32)


def _dot_nt(a, b):
    return lax.dot_general(a, b, (((1,), (1,)), ((), ())), preferred_element_type=F32)


def _dot_tn(a, b):
    return lax.dot_general(a, b, (((0,), (0,)), ((), ())), preferred_element_type=F32)


def _headnorm(x, e_ref):
    x2 = (x * x).astype(BF16)
    w = x.shape[1]
    if w <= MXU_W:
        ms = _dot(x2, e_ref[0:w, 0:w])
    else:
        ms = jnp.concatenate([_dot(x2[:, c:c + MXU_W], e_ref[...]) for c in range(0, w, MXU_W)], axis=1)
    return x * lax.rsqrt(ms + EPS)


def _shift_div(x, d):
    assert d & (d - 1) == 0
    return x >> (d.bit_length() - 1)


def _silu(x):
    return x * (1.0 / (1.0 + jnp.exp(-x)))


def _ffn_first_step(is_first, x_ref, g_ref, wg_hbm, wu_hbm, wd_hbm, o_ref,
                    wg_bf, wu_bf, wd_bf, stage_gu, stage_d, h_scr, sem):
    nch = D_FF // FFN_CHUNK

    def copies(c):
        slot, lo, hi = c % 2, c * FFN_CHUNK, (c + 1) * FFN_CHUNK
        out = []
        for q in range(FFN_COPY_SPLIT):
            r0, r1 = q * D_MODEL // FFN_COPY_SPLIT, (q + 1) * D_MODEL // FFN_COPY_SPLIT
            d0, d1 = q * FFN_CHUNK // FFN_COPY_SPLIT, (q + 1) * FFN_CHUNK // FFN_COPY_SPLIT
            out += [
                pltpu.make_async_copy(wg_hbm.at[r0:r1, lo:hi], stage_gu.at[slot, 0, r0:r1], sem.at[slot, 0, q]),
                pltpu.make_async_copy(wu_hbm.at[r0:r1, lo:hi], stage_gu.at[slot, 1, r0:r1], sem.at[slot, 1, q]),
                pltpu.make_async_copy(wd_hbm.at[lo + d0:lo + d1, :], stage_d.at[slot, d0:d1], sem.at[slot, 2, q]),
            ]
        return out

    def start(c):
        for cp in copies(c):
            cp.start()

    @pl.when(is_first)
    def _start():
        for c in range(2):
            start(c)
        h_scr[...] = _rmsnorm(x_ref[...], g_ref[...]).astype(BF16)

    for c in range(nch):
        @pl.when(is_first)
        def _chunk(c=c):
            slot, lo, hi = c % 2, c * FFN_CHUNK, (c + 1) * FFN_CHUNK
            for cp in copies(c):
                cp.wait()
            wg_bf[:, lo:hi] = stage_gu[slot, 0].astype(BF16)
            wu_bf[:, lo:hi] = stage_gu[slot, 1].astype(BF16)
            wd_bf[lo:hi, :] = stage_d[slot].astype(BF16)
            if c + 2 < nch:
                start(c + 2)
            g = _dot(h_scr[...], wg_bf[:, lo:hi])
            u = _dot(h_scr[...], wu_bf[:, lo:hi])
            part = _dot((_silu(g) * u).astype(BF16), wd_bf[lo:hi, :])
            o_ref[...] = (x_ref[...] if c == 0 else o_ref[...]) + 0.5 * part


def _ffn_kernel(x_ref, g_ref, wg_ref, wu_ref, wd_ref, *rest, n_cast, has_w_in, f32_weights):
    n_in = n_cast + int(has_w_in)
    cast_in, o_ref, cast_out = rest[:n_in], rest[n_in], rest[n_in + 1:2 * n_in + 1]
    scratch = rest[2 * n_in + 1:]

    def tile_step(wg, wu, wd):
        x = x_ref[...]
        h = _rmsnorm(x, g_ref[...]).astype(BF16)
        g = _dot(h, wg[...])
        u = _dot(h, wu[...])
        y = _dot((_silu(g) * u).astype(BF16), wd[...])
        o_ref[...] = x + 0.5 * y

    if f32_weights:
        _ffn_first_step(pl.program_id(0) == 0, x_ref, g_ref, wg_ref, wu_ref, wd_ref, o_ref, *scratch)

        @pl.when(pl.program_id(0) > 0)
        def _rest():
            tile_step(*scratch[:3])
    else:
        tile_step(wg_ref, wu_ref, wd_ref)
    for src, dst in zip(cast_in[:n_cast], cast_out[:n_cast]):
        dst[...] = src[...].astype(BF16)
    if has_w_in:
        src, dst = cast_in[n_cast], cast_out[n_cast]
        w = src[...]
        dst[:, 0:C_MQ] = w[:, 0:C_MQ].astype(BF16)
        dst[:, C_MQ:C_LR] = w[:, C_MQ + GLA_RANK:C_MQ + GLA_RANK + MEM_Q_W].astype(BF16)
        slab = w[:, C_MQ:C_MQ + LANES]
        lane = lax.broadcasted_iota(jnp.int32, slab.shape, 1)
        dst[:, C_LR:C_LR + LANES] = jnp.where(lane < GLA_RANK, slab, 0.0).astype(BF16)


def _const_spec(shape):
    return pl.BlockSpec(shape, lambda *_: (0,) * len(shape), pipeline_mode=pl.Buffered(1))


def _ffn(x2d, gain, wg, wu, wd, *, tm, cast=(), w_in=None):
    t = x2d.shape[0]
    steps = t // tm
    f32_weights = wg.dtype == F32
    assert wu.dtype == wg.dtype and wd.dtype == wg.dtype
    if f32_weights:
        w_specs = [pl.BlockSpec(memory_space=pl.ANY)] * 3
        scratch = [pltpu.VMEM((D_MODEL, D_FF), BF16), pltpu.VMEM((D_MODEL, D_FF), BF16),
                   pltpu.VMEM((D_FF, D_MODEL), BF16),
                   pltpu.VMEM((2, 2, D_MODEL, FFN_CHUNK), F32),
                   pltpu.VMEM((2, FFN_CHUNK, D_MODEL), F32),
                   pltpu.VMEM((tm, D_MODEL), BF16),
                   pltpu.SemaphoreType.DMA((2, 3, FFN_COPY_SPLIT))]
    else:
        w_specs = [_const_spec((D_MODEL, D_FF)), _const_spec((D_MODEL, D_FF)), _const_spec((D_FF, D_MODEL))]
        scratch = []
    extra_in, extra_specs, extra_out_specs, extra_out_shapes = [], [], [], []
    for w in tuple(cast) + ((w_in,) if w_in is not None else ()):
        rows, cols = w.shape
        nblk = steps
        while rows % nblk or (rows // nblk) % BF16_ROWS:
            nblk //= 2
        blk = rows // nblk
        out_cols = IN_W_PAD if w is w_in else cols
        extra_in.append(w)
        index_map = functools.partial(lambda i, n: (jnp.minimum(i, n - 1), 0), n=nblk)
        extra_specs.append(pl.BlockSpec((blk, cols), index_map))
        extra_out_specs.append(pl.BlockSpec((blk, out_cols), index_map))
        extra_out_shapes.append(jax.ShapeDtypeStruct((rows, out_cols), BF16))
    outs = pl.pallas_call(
        functools.partial(_ffn_kernel, n_cast=len(cast), has_w_in=w_in is not None,
                          f32_weights=f32_weights),
        grid=(steps,),
        in_specs=[pl.BlockSpec((tm, D_MODEL), lambda i: (i, 0)), _const_spec((1, D_MODEL))]
        + w_specs + extra_specs,
        out_specs=[pl.BlockSpec((tm, D_MODEL), lambda i: (i, 0))] + extra_out_specs,
        out_shape=[jax.ShapeDtypeStruct((t, D_MODEL), F32)] + extra_out_shapes,
        scratch_shapes=scratch,
        compiler_params=pltpu.CompilerParams(
            dimension_semantics=("arbitrary",), vmem_limit_bytes=VMEM_LIMIT),
        name="ffn_halfstep",
    )(x2d, gain, wg, wu, wd, *extra_in)
    return outs[0], outs[1:]


def _mix_kernel(*refs, tiles_per_seq, n_tiles):
    step = pl.program_id(0)

    @pl.when(step < n_tiles)
    def _steady():
        _mix_step(*refs, tiles_per_seq=tiles_per_seq)

    @pl.when(step == n_tiles)
    def _drain():
        r = dict(zip(_mix_step.__code__.co_varnames, refs))
        r["o_ref"][...] = r["xp_ref"][...] + _dot(r["yprev_scr"][...], r["wout_ref"][...])


def _mix_step(sinks_ref, relb_ref,
              xp_ref, xn_ref, mem_ref, mixg_ref, memg_ref, win_ref, wmem_ref, wout_ref,
              qg_ref, kg_ref, mqg_ref, mkg_ref, og_ref, wgu_ref, bg_ref, bkt_ref,
              o_ref,
              bias_scr, e_scr, tri_scr, ind_scr, smask_scr,
              proj_scr, kext_scr, vext_scr, mk_scr, mv_scr, st_scr, y_scr, yprev_scr,
              h_scr, gq_scr, gk_scr, gv_scr, gate_scr, b_scr, go_scr,
              *, tiles_per_seq):
    step = pl.program_id(0)
    first_of_seq = lax.rem(step, tiles_per_seq) == 0
    tq = MIX_TQ
    nblk = tq // BLOCK
    nchunk = tq // GLA_CHUNK

    @pl.when(step == 0)
    def _prologue():
        bkt = bkt_ref[...]
        for h in range(SWA_HEADS):
            acc = jnp.zeros((BLOCK, 2 * BLOCK), F32)
            for k in range(N_BUCKETS):
                acc = jnp.where(bkt == k, relb_ref[k * SWA_HEADS + h] * LOG2E, acc)
            bias_scr[h] = jnp.where(bkt >= 0, acc, -jnp.inf)
        r = lax.broadcasted_iota(jnp.int32, (MXU_W, MXU_W), 0)
        c = lax.broadcasted_iota(jnp.int32, (MXU_W, MXU_W), 1)
        same_head = _shift_div(r, HEAD_DIM) == _shift_div(c, HEAD_DIM)
        e_scr[...] = jnp.where(same_head, 1.0 / HEAD_DIM, 0.0).astype(BF16)
        same_chunk = _shift_div(r, GLA_CHUNK) == _shift_div(c, GLA_CHUNK)
        tri_scr[...] = jnp.where(same_chunk & (c <= r), 1.0, 0.0).astype(BF16)
        r = lax.broadcasted_iota(jnp.int32, (GLA_QK_W, GLA_V_W), 0)
        c = lax.broadcasted_iota(jnp.int32, (GLA_QK_W, GLA_V_W), 1)
        same_head = _shift_div(r, GLA_DK) == _shift_div(c, GLA_DV)
        ind_scr[...] = jnp.where(same_head, 1.0, 0.0).astype(BF16)
        smask_scr[...] = jnp.where(same_head, 1.0, 0.0).astype(F32)
        yprev_scr[...] = jnp.zeros(yprev_scr.shape, BF16)
        h0 = _rmsnorm(xp_ref[...], mixg_ref[...]).astype(BF16)
        proj_scr[...] = _dot(h0, win_ref[...])

    @pl.when(first_of_seq)
    def _init_sequence():
        kext_scr[0:BLOCK, :] = jnp.zeros((BLOCK, SWA_KV_W), F32)
        vext_scr[0:BLOCK, :] = jnp.zeros((BLOCK, SWA_KV_W), F32)
        st_scr[...] = jnp.zeros(st_scr.shape, F32)
        hm = _rmsnorm(mem_ref[...], memg_ref[...]).astype(BF16)
        kv = _dot(hm, wmem_ref[...])
        mk = _headnorm(kv[:, :MEM_Q_W], e_scr) * mkg_ref[...]
        mk_scr[...] = mk.astype(BF16)
        mv_scr[...] = kv[:, MEM_Q_W:].astype(BF16)

    def output_piece(c0):
        o_ref[:, c0:c0 + PIECE_W] = (xp_ref[:, c0:c0 + PIECE_W]
                                     + _dot(yprev_scr[...], wout_ref[:, c0:c0 + PIECE_W]))

    out_pieces = [functools.partial(output_piece, c0) for c0 in range(0, D_MODEL, PIECE_W)]

    out_pieces.pop(0)()
    qn = _headnorm(proj_scr[:, C_SQ:C_SQ + SWA_Q_W], e_scr) * qg_ref[...]
    qn = (qn * (HEAD_DIM ** -0.5 * LOG2E)).astype(BF16)
    out_pieces.pop(0)()
    kn = _headnorm(proj_scr[:, C_SK:C_SK + SWA_KV_W], e_scr) * kg_ref[...]
    kext_scr[BLOCK:BLOCK + tq, :] = kn
    vext_scr[BLOCK:BLOCK + tq, :] = proj_scr[:, C_SV:C_SV + SWA_KV_W]
    mqn = _headnorm(proj_scr[:, C_MQ:C_MQ + MEM_Q_W], e_scr) * mqg_ref[...]
    mqn = (mqn * (HEAD_DIM ** -0.5 * LOG2E)).astype(BF16)
    out_pieces.pop(0)()
    z = _dot(proj_scr[:, C_LR:C_LR + LANES].astype(BF16), wgu_ref[...]) + bg_ref[...]
    log_a = (jnp.minimum(z, 0.0) - jnp.log(1.0 + jnp.exp(-jnp.abs(z)))) * (1.0 / GLA_TAU)
    la_hi = log_a.astype(BF16)
    la_lo = (log_a - la_hi.astype(F32)).astype(BF16)
    la2 = jnp.concatenate([la_hi, la_lo], axis=1)
    for r0 in range(0, tq, MXU_W):
        cs = _dot(tri_scr[...], la2[r0:r0 + MXU_W])
        b_scr[r0:r0 + MXU_W, :] = (cs[:, 0:LANES] + cs[:, LANES:2 * LANES]) * LOG2E
    out_pieces.pop(0)()
    assert not out_pieces
    gq_scr[...] = proj_scr[:, C_GQ:C_GQ + GLA_QK_W] * (GLA_DK ** -0.5)
    gk_scr[...] = proj_scr[:, C_GK:C_GK + GLA_QK_W]
    gv_scr[...] = proj_scr[:, C_GV:C_GV + GLA_V_W]
    gate_scr[...] = proj_scr[:, C_GG:C_GG + GLA_V_W]

    h_scr[...] = _rmsnorm(xn_ref[...], mixg_ref[...]).astype(BF16)

    def project_piece(c0):
        c1 = min(c0 + PIECE_W, IN_W_PAD)
        proj_scr[:, c0:c1] = _dot(h_scr[...], win_ref[:, c0:c1])

    pieces = [functools.partial(project_piece, c0) for c0 in range(0, IN_W_PAD, PIECE_W)]

    def issue_pieces(n):
        for _ in range(n):
            if pieces:
                pieces.pop(0)()

    lane_lo = lax.broadcasted_iota(jnp.int32, (BLOCK, LANES), 1) < HEAD_DIM
    k_a = kext_scr[...]
    v_a = vext_scr[...]
    k_b = pltpu.roll(k_a, HEAD_DIM, 1)
    v_b = pltpu.roll(v_a, HEAD_DIM, 1)
    ext_lo = lax.broadcasted_iota(jnp.int32, k_a.shape, 1) < HEAD_DIM
    kk = (jnp.where(ext_lo, k_a, k_b).astype(BF16), jnp.where(ext_lo, k_b, k_a).astype(BF16))
    vv = (jnp.where(ext_lo, v_a, v_b).astype(BF16), jnp.where(ext_lo, v_b, v_a).astype(BF16))
    col = lax.broadcasted_iota(jnp.int32, (BLOCK, 2 * BLOCK), 1)
    zero_q = jnp.zeros((BLOCK, LANES), BF16)

    for nb in range(nblk):
        r0 = nb * BLOCK
        for hk in range(SWA_KV_HEADS):
            kband = kk[hk][r0:r0 + 2 * BLOCK]
            vband = vv[hk][r0:r0 + 2 * BLOCK]
            qparts = []
            for pair in range(2):
                c0 = hk * SWA_GROUP * HEAD_DIM + pair * LANES
                qp = qn[r0:r0 + BLOCK, c0:c0 + LANES]
                qparts.append(jnp.where(lane_lo, qp, zero_q))
                qparts.append(jnp.where(lane_lo, zero_q, qp))
            s_all = _dot_nt(jnp.concatenate(qparts, axis=0), kband)
            ps, inv = [], []
            for g in range(SWA_GROUP):
                head = hk * SWA_GROUP + g
                s = s_all[g * BLOCK:(g + 1) * BLOCK] + bias_scr[head]
                if nb == 0:
                    s = jnp.where((col >= BLOCK) | jnp.logical_not(first_of_seq), s, -jnp.inf)
                sink = sinks_ref[head] * LOG2E
                m = jnp.maximum(jnp.max(s, axis=-1, keepdims=True), sink)
                p = jnp.exp2(s - m)
                den = jnp.sum(p, axis=-1, keepdims=True) + jnp.exp2(sink - m)
                ps.append(p.astype(BF16))
                inv.append(1.0 / den)
            o_all = _dot(jnp.concatenate(ps, axis=0), vband)
            for pair in range(2):
                g0 = 2 * pair
                o_pair = jnp.where(lane_lo, o_all[g0 * BLOCK:(g0 + 1) * BLOCK] * inv[g0],
                                   o_all[(g0 + 1) * BLOCK:(g0 + 2) * BLOCK] * inv[g0 + 1])
                c0 = hk * SWA_GROUP * HEAD_DIM + pair * LANES
                y_scr[r0:r0 + BLOCK, c0:c0 + LANES] = o_pair.astype(BF16)

    kext_scr[0:BLOCK, :] = kext_scr[tq:tq + BLOCK, :]
    vext_scr[0:BLOCK, :] = vext_scr[tq:tq + BLOCK, :]

    lane_lo_t = lax.broadcasted_iota(jnp.int32, (tq, LANES), 1) < HEAD_DIM
    zero_t = jnp.zeros((tq, LANES), BF16)
    for pair in range(MEM_HEADS // 2):
        issue_pieces(1)
        qp = mqn[:, pair * LANES:(pair + 1) * LANES]
        qst = jnp.concatenate([jnp.where(lane_lo_t, qp, zero_t), jnp.where(lane_lo_t, zero_t, qp)], axis=0)
        s = _dot_nt(qst, mk_scr[:, pair * LANES:(pair + 1) * LANES])
        m = jnp.max(s, axis=-1, keepdims=True)
        p = jnp.exp2(s - m)
        rinv = 1.0 / jnp.sum(p, axis=-1, keepdims=True)
        o2 = _dot(p.astype(BF16), mv_scr[:, pair * LANES:(pair + 1) * LANES]) * rinv
        o_pair = jnp.where(lane_lo_t, o2[:tq], o2[tq:])
        c0 = SWA_Q_W + GLA_V_W + pair * LANES
        y_scr[:, c0:c0 + LANES] = o_pair.astype(BF16)

    sub_row = lax.broadcasted_iota(jnp.int32, (SUBLANES, GLA_QK_W), 0)
    nsub = GLA_CHUNK // SUBLANES
    eye_qk = (lax.broadcasted_iota(jnp.int32, (GLA_QK_W, GLA_QK_W), 0)
              == lax.broadcasted_iota(jnp.int32, (GLA_QK_W, GLA_QK_W), 1))

    for c in range(nchunk):
        if c % 3 == 0:
            issue_pieces(1)
        r = c * GLA_CHUNK
        bc = b_scr[r:r + GLA_CHUNK, :]
        qc = gq_scr[r:r + GLA_CHUNK, :]
        kc = gk_scr[r:r + GLA_CHUNK, :]
        vc = gv_scr[r:r + GLA_CHUNK, :]
        qs = [qc[m * SUBLANES:(m + 1) * SUBLANES] for m in range(nsub)]
        bs = [bc[m * SUBLANES:(m + 1) * SUBLANES] for m in range(nsub)]
        terms = []
        for j in range(GLA_CHUNK):
            kj = kc[j:j + 1, :]
            bj = bc[j:j + 1, :]
            for m in range(j // SUBLANES, nsub):
                diff = bs[m] - bj
                if m == j // SUBLANES and j % SUBLANES:
                    diff = jnp.where(sub_row >= j % SUBLANES, diff, -jnp.inf)
                terms.append(qs[m] * kj * jnp.exp2(diff))
        a_b = _dot(jnp.concatenate(terms, axis=0).astype(BF16), ind_scr[...])
        acc = [None] * nsub
        t = 0
        for j in range(GLA_CHUNK):
            vj = vc[j:j + 1, :]
            for m in range(j // SUBLANES, nsub):
                contrib = a_b[t * SUBLANES:(t + 1) * SUBLANES] * vj
                acc[m] = contrib if acc[m] is None else acc[m] + contrib
                t += 1
        o_intra = jnp.concatenate(acc, axis=0)
        b_last = bc[GLA_CHUNK - 1:GLA_CHUNK, :]
        st = st_scr[...]
        o_inter = _dot((qc * jnp.exp2(bc)).astype(BF16), st.astype(BF16))
        kd = (kc * jnp.exp2(b_last - bc)).astype(BF16)
        kv = _dot_tn(kd, vc.astype(BF16))
        b_col = jnp.sum(jnp.where(eye_qk, b_last, 0.0), axis=1, keepdims=True)
        st_scr[...] = jnp.exp2(b_col) * st + kv * smask_scr[...]
        go_scr[r:r + GLA_CHUNK, :] = o_intra + o_inter

    issue_pieces(len(pieces))
    on = _headnorm(go_scr[...], e_scr) * og_ref[...]
    y_scr[:, SWA_Q_W:SWA_Q_W + GLA_V_W] = (on * _silu(gate_scr[...])).astype(BF16)

    yprev_scr[...] = y_scr[...]


def _mix(x2d, mem, mix_g, mem_g, w_in_r, w_mem, w_out, qg, kg, mqg, mkg, og, wgu, bg,
         sinks, relb, bkt, *, seq):
    tq = MIX_TQ
    ntiles = x2d.shape[0] // tq
    tiles_per_seq = seq // tq
    last = ntiles - 1
    smem = pl.BlockSpec(memory_space=pltpu.SMEM)
    in_specs = [
        smem, smem,
        pl.BlockSpec((tq, D_MODEL), lambda s: (jnp.maximum(s - 1, 0), 0)),
        pl.BlockSpec((tq, D_MODEL), lambda s: (jnp.minimum(s + 1, last), 0)),
        pl.BlockSpec((None, MEM_LEN, D_MODEL), lambda s: (jnp.minimum(s, last) // tiles_per_seq, 0, 0)),
        _const_spec((1, D_MODEL)), _const_spec((1, D_MODEL)),
        _const_spec((D_MODEL, IN_W_PAD)),
        _const_spec((D_MODEL, 2 * MEM_Q_W)),
        _const_spec((MIX_W, D_MODEL)),
        _const_spec((1, SWA_Q_W)), _const_spec((1, SWA_KV_W)),
        _const_spec((1, MEM_Q_W)), _const_spec((1, MEM_Q_W)), _const_spec((1, GLA_V_W)),
        _const_spec((LANES, GLA_QK_W)), _const_spec((1, GLA_QK_W)),
        _const_spec((BLOCK, 2 * BLOCK)),
    ]
    scratch = [
        pltpu.VMEM((SWA_HEADS, BLOCK, 2 * BLOCK), F32),
        pltpu.VMEM((MXU_W, MXU_W), BF16),
        pltpu.VMEM((MXU_W, MXU_W), BF16),
        pltpu.VMEM((GLA_QK_W, GLA_V_W), BF16),
        pltpu.VMEM((GLA_QK_W, GLA_V_W), F32),
        pltpu.VMEM((tq, IN_W_PAD), F32),
        pltpu.VMEM((BLOCK + tq, SWA_KV_W), F32),
        pltpu.VMEM((BLOCK + tq, SWA_KV_W), F32),
        pltpu.VMEM((MEM_LEN, MEM_Q_W), BF16),
        pltpu.VMEM((MEM_LEN, MEM_Q_W), BF16),
        pltpu.VMEM((GLA_QK_W, GLA_V_W), F32),
        pltpu.VMEM((tq, MIX_W), BF16),
        pltpu.VMEM((tq, MIX_W), BF16),
        pltpu.VMEM((tq, D_MODEL), BF16),
        pltpu.VMEM((tq, GLA_QK_W), F32),
        pltpu.VMEM((tq, GLA_QK_W), F32),
        pltpu.VMEM((tq, GLA_V_W), F32),
        pltpu.VMEM((tq, GLA_V_W), F32),
        pltpu.VMEM((tq, GLA_QK_W), F32),
        pltpu.VMEM((tq, GLA_V_W), F32),
    ]
    return pl.pallas_call(
        functools.partial(_mix_kernel, tiles_per_seq=tiles_per_seq, n_tiles=ntiles),
        grid=(ntiles + 1,),
        in_specs=in_specs,
        out_specs=pl.BlockSpec((tq, D_MODEL), lambda s: (jnp.maximum(s - 1, 0), 0)),
        out_shape=jax.ShapeDtypeStruct(x2d.shape, F32),
        scratch_shapes=scratch,
        compiler_params=pltpu.CompilerParams(
            dimension_semantics=("arbitrary",), vmem_limit_bytes=VMEM_LIMIT),
        name="mixing",
    )(sinks, relb, x2d, x2d, mem, mix_g, mem_g, w_in_r, w_mem, w_out, qg, kg, mqg, mkg, og,
      wgu, bg, bkt)


def _tile_gain(g, reps):
    return jnp.tile(g.astype(F32), reps).reshape(1, -1)


def kernel(x, mem, ffn1_norm, ffn1_w_gate, ffn1_w_up, ffn1_w_down, mix_norm, mem_norm, w_in, w_mem_kv, swa_q_norm, swa_k_norm, swa_sinks, rel_bias, gla_w_gate_up, gla_b_gate, gla_out_norm, mem_q_norm, mem_k_norm, w_out, ffn2_norm, ffn2_w_gate, ffn2_w_up, ffn2_w_down):
    bsz, seq, d = x.shape
    depth = ffn1_norm.shape[0]
    bkt = jnp.asarray(_t5_bucket_table())
    relb = rel_bias.astype(F32).reshape(-1)
    x2d = x.reshape(bsz * seq, d)
    for l in range(depth):
        x2d, (wg2, wu2, wd2, w_out_b, w_mem_b, w_in_r) = _ffn(
            x2d, ffn1_norm[l].reshape(1, d),
            ffn1_w_gate[l], ffn1_w_up[l], ffn1_w_down[l], tm=FFN_TM_FIRST,
            cast=(ffn2_w_gate[l], ffn2_w_up[l], ffn2_w_down[l], w_out[l], w_mem_kv[l]), w_in=w_in[l])
        wgu = jnp.concatenate(
            [gla_w_gate_up[l], jnp.zeros((LANES - GLA_RANK, GLA_QK_W), gla_w_gate_up.dtype)],
            axis=0).astype(BF16)
        x2d = _mix(x2d, mem, mix_norm[l].reshape(1, d), mem_norm[l].reshape(1, d),
                   w_in_r, w_mem_b, w_out_b,
                   _tile_gain(swa_q_norm[l], SWA_HEADS), _tile_gain(swa_k_norm[l], SWA_KV_HEADS),
                   _tile_gain(mem_q_norm[l], MEM_HEADS), _tile_gain(mem_k_norm[l], MEM_HEADS),
                   _tile_gain(gla_out_norm[l], GLA_HEADS), wgu, gla_b_gate[l].reshape(1, -1),
                   swa_sinks[l].astype(F32), relb, bkt, seq=seq)
        x2d, _ = _ffn(x2d, ffn2_norm[l].reshape(1, d), wg2, wu2, wd2, tm=FFN_TM)
    return x2d.reshape(bsz, seq, d)
```
